```python
import jax, jax.numpy as jnp
from jax import lax
import numpy as np

D_MODEL = 1024
BATCH = 16
SEQ = 256
DEPTH = 4
DEC_BATCH = 2
DEC_SEQ = 2048
PAST_LEN = 512

GRID_W = 64
N_MIXERS = 3
N_MLSTM_LAYERS = (DEPTH + 2) // 3
N_MLA_LAYERS = (DEPTH + 1) // 3
N_FOURIER_LAYERS = DEPTH // 3
RMS_EPS = 1e-6

ML_HEADS = 8
ML_DV = D_MODEL // ML_HEADS
ML_DK = ML_DV // 2
ML_CHUNK = 64
ML_FGATE_BIAS = 3.0

MLA_HEADS = 16
QK_NOPE = 128
QK_ROPE = 64
V_HEAD = 128
Q_LORA = 384
KV_LORA = 256
ROPE_THETA = 10000.0
Q_BLOCK = 128

FN_GROUPS = 4

N_EXPERTS = 32
TOP_K = 4
D_EXPERT = D_MODEL
SWIGLU_ALPHA = 1.702
SWIGLU_LIMIT = 7.0
MOE_BLOCK = 128

kernel_name = "hybrid_mlstm_mla_fnet_moe_diffusion_step"


def rmsnorm(x, g):
    xf = x.astype(jnp.float32)
    y = xf * lax.rsqrt(jnp.mean(xf * xf, axis=-1, keepdims=True) + RMS_EPS)
    return y.astype(x.dtype) * g


def modulation(cond, w, b):
    return jnp.split(jax.nn.silu(cond) @ w + b, 6, axis=-1)


def axial_rope_tables(n_tokens):
    rows = n_tokens // GRID_W
    row = jnp.repeat(jnp.arange(rows, dtype=jnp.float32), GRID_W)
    col = jnp.tile(jnp.arange(GRID_W, dtype=jnp.float32), rows)
    n_freq = QK_ROPE // 4
    inv = ROPE_THETA ** (-jnp.arange(n_freq, dtype=jnp.float32) / n_freq)
    ang = jnp.stack([row[:, None] * inv, col[:, None] * inv], axis=1)
    return jnp.cos(ang), jnp.sin(ang)


def apply_axial_rope(x, cos, sin):
    xr = x.astype(jnp.float32).reshape(x.shape[:-1] + (2, 2, QK_ROPE // 4))
    x1, x2 = xr[..., 0, :], xr[..., 1, :]
    out = jnp.stack([x1 * cos - x2 * sin, x2 * cos + x1 * sin], axis=-2)
    return out.reshape(x.shape).astype(x.dtype)


def block_attention(q, k, v):
    B, Tq, H, dq = q.shape
    nb = Tq // Q_BLOCK
    scale = (QK_NOPE + QK_ROPE) ** -0.5
    qb = jnp.moveaxis(q.reshape(B, nb, Q_BLOCK, H, dq), 1, 0)

    def one_block(qblk):
        s = jnp.einsum('bqhd,bkhd->bhqk', qblk, k).astype(jnp.float32) * scale
        p = jax.nn.softmax(s, axis=-1)
        return jnp.einsum('bhqk,bkhv->bqhv', p.astype(v.dtype), v)

    o = lax.map(one_block, qb)
    return jnp.moveaxis(o, 0, 1).reshape(B, Tq, H, v.shape[-1])


def mlstm_chunkwise(q, k, v, log_i, log_f, C0, n0, m0):
    B, T, H, _ = q.shape
    nc = T // ML_CHUNK

    def to_chunks(a):
        a = a.reshape((B, nc, ML_CHUNK) + a.shape[2:])
        return jnp.moveaxis(a, (1, 3), (0, 2))

    mask = jnp.tril(jnp.ones((ML_CHUNK, ML_CHUNK), dtype=bool))

    def step(carry, xs):
        C, n, m = carry
        qc, kc, vc, ic, fc = xs
        b = jnp.cumsum(fc, axis=-1)
        a = b + m[..., None]
        dmat = jnp.where(mask, b[..., :, None] - b[..., None, :] + ic[..., None, :], -jnp.inf)
        m_row = jnp.maximum(a, jnp.max(dmat, axis=-1))
        w_intra = jnp.exp(dmat - m_row[..., None])
        w_inter = jnp.exp(a - m_row)
        s = jnp.einsum('bhjd,bhsd->bhjs', qc, kc) * w_intra
        num = jnp.einsum('bhjs,bhsv->bhjv', s, vc) + w_inter[..., None] * jnp.einsum('bhjd,bhdv->bhjv', qc, C)
        den = jnp.sum(s, axis=-1) + w_inter * jnp.einsum('bhjd,bhd->bhj', qc, n)
        h = num / jnp.maximum(jnp.abs(den), jnp.exp(-m_row))[..., None]
        b_last = b[..., -1]
        g = b_last[..., None] - b + ic
        m_new = jnp.maximum(b_last + m, jnp.max(g, axis=-1))
        wk = jnp.exp(g - m_new[..., None])
        decay = jnp.exp(b_last + m - m_new)
        C_new = decay[..., None, None] * C + jnp.einsum('bhs,bhsd,bhsv->bhdv', wk, kc, vc)
        n_new = decay[..., None] * n + jnp.einsum('bhs,bhsd->bhd', wk, kc)
        return (C_new, n_new, m_new), h

    xs = (to_chunks(q), to_chunks(k), to_chunks(v), to_chunks(log_i), to_chunks(log_f))
    state, hs = lax.scan(step, (C0, n0, m0), xs)
    h = jnp.moveaxis(hs, (0, 2), (1, 3)).reshape(B, T, H, v.shape[-1])
    return h, state


def mlstm_mix(h, w_in, gate_b, norm_g, w_out, init_f, init_b):
    B, T, _ = h.shape
    hk, hv = ML_HEADS * ML_DK, ML_HEADS * ML_DV
    q, k, v, o, g = jnp.split(h @ w_in, [hk, 2 * hk, 2 * hk + hv, 2 * hk + 2 * hv], axis=-1)
    q = q.reshape(B, T, ML_HEADS, ML_DK).astype(jnp.float32) * ML_DK ** -0.5
    k = k.reshape(B, T, ML_HEADS, ML_DK).astype(jnp.float32)
    v = v.reshape(B, T, ML_HEADS, ML_DV).astype(jnp.float32)
    g = g.reshape(B, T, 4, ML_HEADS).astype(jnp.float32) + gate_b.astype(jnp.float32)
    i_f, i_b = g[:, :, 0], g[:, :, 1]
    lf_f, lf_b = jax.nn.log_sigmoid(g[:, :, 2]), jax.nn.log_sigmoid(g[:, :, 3])
    rev = lambda a: jnp.flip(a, axis=1)
    h_f, st_f = mlstm_chunkwise(q, k, v, i_f, lf_f, *init_f)
    h_b, st_b = mlstm_chunkwise(rev(q), rev(k), rev(v), rev(i_b), rev(lf_b), *init_b)
    hh = rmsnorm(h_f + rev(h_b), norm_g.reshape(ML_HEADS, ML_DV))
    out = (jax.nn.sigmoid(o.astype(jnp.float32)) * hh.reshape(B, T, hv)).astype(h.dtype) @ w_out
    return out, st_f, st_b


def mla_project(h, w_in, q_norm_g, w_uq, kv_norm_g):
    B, T, _ = h.shape
    cq, ckv, kpe = jnp.split(h @ w_in, [Q_LORA, Q_LORA + KV_LORA], axis=-1)
    q = (rmsnorm(cq, q_norm_g) @ w_uq).reshape(B, T, MLA_HEADS, QK_NOPE + QK_ROPE)
    return q, rmsnorm(ckv, kv_norm_g), kpe


def mla_expand(ckv, kpe, w_ukv):
    B, T, _ = ckv.shape
    kv = (ckv @ w_ukv).reshape(B, T, MLA_HEADS, QK_NOPE + V_HEAD)
    k = jnp.concatenate([kv[..., :QK_NOPE], jnp.broadcast_to(kpe[:, :, None, :], (B, T, MLA_HEADS, QK_ROPE))], axis=-1)
    return k, kv[..., QK_NOPE:]


def fourier_mix(h, w_out):
    B, T, D = h.shape
    hg = h.astype(jnp.float32).reshape(B, T, FN_GROUPS, D // FN_GROUPS)
    f = jnp.fft.fft2(hg, axes=(1, 3), norm='ortho').real
    return f.reshape(B, T, D).astype(h.dtype) @ w_out


def moe_ffn(x, router_w, router_b, w1, b1, w2, b2):
    B, T, D = x.shape
    xt = x.reshape(-1, D)
    n_tok = xt.shape[0]
    logits = (xt @ router_w).astype(jnp.float32) + router_b.astype(jnp.float32)
    top_val, top_idx = lax.top_k(logits, TOP_K)
    gates = jax.nn.softmax(top_val, axis=-1)
    n_assign = n_tok * TOP_K
    exp_ids = top_idx.reshape(-1).astype(jnp.int32)
    tok_ids = jnp.arange(n_assign, dtype=jnp.int32) // TOP_K
    order = jnp.argsort(exp_ids)
    s_exp, s_tok, s_gate = exp_ids[order], tok_ids[order], gates.reshape(-1)[order]
    counts = jnp.bincount(exp_ids, length=N_EXPERTS).astype(jnp.int32)
    starts = jnp.cumsum(counts) - counts
    pcounts = (counts + MOE_BLOCK - 1) // MOE_BLOCK * MOE_BLOCK
    pends = jnp.cumsum(pcounts)
    pstarts = pends - pcounts
    dest = pstarts[s_exp] + jnp.arange(n_assign, dtype=jnp.int32) - starts[s_exp]
    n_blocks = -(-n_assign // MOE_BLOCK) + N_EXPERTS
    slot_tok = jnp.full((n_blocks * MOE_BLOCK,), n_tok, dtype=jnp.int32).at[dest].set(s_tok)
    block_exp = jnp.minimum(jnp.searchsorted(pends, jnp.arange(n_blocks) * MOE_BLOCK, side='right'), N_EXPERTS - 1)
    x_pad = jnp.concatenate([xt, jnp.zeros((1, D), xt.dtype)], axis=0)
    xb = x_pad[slot_tok].reshape(n_blocks, MOE_BLOCK, D)

    def expert_block(args):
        xblk, e = args
        gate, up = jnp.split(xblk @ w1[e] + b1[e], 2, axis=-1)
        gate = jnp.minimum(gate, SWIGLU_LIMIT)
        up = jnp.clip(up, -SWIGLU_LIMIT, SWIGLU_LIMIT)
        act = (up + 1) * (gate * jax.nn.sigmoid(SWIGLU_ALPHA * gate))
        return act @ w2[e] + b2[e]

    yb = lax.map(expert_block, (xb, block_exp)).reshape(-1, D)
    contrib = yb[dest] * s_gate[:, None].astype(yb.dtype)
    out = jnp.zeros_like(xt).at[s_tok].add(contrib.astype(xt.dtype))
    return out.reshape(B, T, D)


def run_trunk(x, cond, p, caches):
    B, T, _ = x.shape
    f32 = jnp.float32
    is_ctx = caches is None
    saved = ([], [], [], [], [])
    if not is_ctx:
        cos, sin = axial_rope_tables(T)
    for l in range(DEPTH):
        kind, j = l % N_MIXERS, l // N_MIXERS
        sh1, sc1, g1, sh2, sc2, g2 = modulation(cond, p['ada_w'][l], p['ada_b'][l])
        h = rmsnorm(x, p['norm1_g'][l]) * (1 + sc1) + sh1
        if kind == 0:
            if is_ctx:
                zero = (jnp.zeros((B, ML_HEADS, ML_DK, ML_DV), f32),
                        jnp.zeros((B, ML_HEADS, ML_DK), f32),
                        jnp.zeros((B, ML_HEADS), f32))
                init_f = init_b = zero
            else:
                C0 = caches[2][:, j].astype(f32)
                n0 = caches[3][:, j].astype(f32)
                m0 = caches[4][:, j].astype(f32)
                init_f = (C0[:, 0], n0[:, 0], m0[:, 0])
                init_b = (C0[:, 1], n0[:, 1], m0[:, 1])
            out, st_f, st_b = mlstm_mix(h, p['ml_w_in'][j], p['ml_gate_b'][j], p['ml_norm_g'][j],
                                        p['ml_w_out'][j], init_f, init_b)
            if is_ctx:
                for slot, sf, sb in zip(saved[2:], st_f, st_b):
                    slot.append(jnp.stack([sf, sb], axis=1).astype(x.dtype))
        elif kind == 1:
            q, ckv, kpe = mla_project(h, p['mla_w_in'][j], p['mla_q_norm_g'][j], p['mla_w_uq'][j],
                                      p['mla_kv_norm_g'][j])
            w_ukv = p['mla_w_ukv'][j]
            if is_ctx:
                k, v = mla_expand(ckv, kpe, w_ukv)
                saved[0].append(ckv)
                saved[1].append(kpe)
            else:
                q = jnp.concatenate([q[..., :QK_NOPE],
                                     apply_axial_rope(q[..., QK_NOPE:], cos[:, None], sin[:, None])], axis=-1)
                k_lat, v_lat = mla_expand(ckv, apply_axial_rope(kpe, cos, sin), w_ukv)
                k_ctx, v_ctx = mla_expand(caches[0][:, j].astype(x.dtype), caches[1][:, j].astype(x.dtype), w_ukv)
                k = jnp.concatenate([k_ctx, k_lat], axis=1)
                v = jnp.concatenate([v_ctx, v_lat], axis=1)
            out = block_attention(q, k, v).reshape(B, T, MLA_HEADS * V_HEAD) @ p['mla_w_out'][j]
        else:
            out = fourier_mix(h, p['fn_w_out'][j])
        x = x + g1 * out
        h = rmsnorm(x, p['norm2_g'][l]) * (1 + sc2) + sh2
        x = x + g2 * moe_ffn(h, p['router_w'][l], p['router_b'][l], p['exp_w1'][l], p['exp_b1'][l],
                             p['exp_w2'][l], p['exp_b2'][l])
    return rmsnorm(x, p['final_g']), saved


def setup_inputs(seed: int = 0) -> dict:
    key = jax.random.key(seed)
    ks = jax.random.split(key, 32)

    def nrm(i, shape, scale=1.0):
        return scale * jax.random.normal(ks[i], shape, jnp.float32)

    hk, hv = ML_HEADS * ML_DK, ML_HEADS * ML_DV
    ml_in_cols = 2 * hk + 2 * hv + 4 * ML_HEADS
    ml_gate_b = jnp.concatenate([nrm(14, (N_MLSTM_LAYERS, 2, ML_HEADS), 0.1),
                                 ML_FGATE_BIAS + nrm(15, (N_MLSTM_LAYERS, 2, ML_HEADS), 0.5)], axis=1)
    return {
        'x_prompt': nrm(0, (BATCH, SEQ, D_MODEL)),
        'x_sample': nrm(1, (DEC_BATCH, DEC_SEQ, D_MODEL)),
        'cache_mla_ckv': nrm(2, (DEC_BATCH, N_MLA_LAYERS, PAST_LEN, KV_LORA)),
        'cache_mla_kpe': nrm(3, (DEC_BATCH, N_MLA_LAYERS, PAST_LEN, QK_ROPE)),
        'state_mlstm_C': nrm(4, (DEC_BATCH, N_MLSTM_LAYERS, 2, ML_HEADS, ML_DK, ML_DV), 0.3),
        'state_mlstm_n': nrm(5, (DEC_BATCH, N_MLSTM_LAYERS, 2, ML_HEADS, ML_DK), 0.3),
        'state_mlstm_m': nrm(6, (DEC_BATCH, N_MLSTM_LAYERS, 2, ML_HEADS)),
        'c': nrm(7, (DEC_BATCH, D_MODEL)),
        'c_ctx': nrm(8, (D_MODEL,)),
        'norm1_g': 1.0 + nrm(9, (DEPTH, D_MODEL), 0.02),
        'norm2_g': 1.0 + nrm(10, (DEPTH, D_MODEL), 0.02),
        'ada_w': nrm(11, (DEPTH, D_MODEL, 6 * D_MODEL), 0.5 * D_MODEL ** -0.5),
        'ada_b': nrm(12, (DEPTH, 6 * D_MODEL), 0.02),
        'ml_w_in': nrm(13, (N_MLSTM_LAYERS, D_MODEL, ml_in_cols), D_MODEL ** -0.5),
        'ml_gate_b': ml_gate_b,
        'ml_norm_g': 1.0 + nrm(16, (N_MLSTM_LAYERS, hv), 0.02),
        'ml_w_out': nrm(17, (N_MLSTM_LAYERS, hv, D_MODEL), hv ** -0.5),
        'mla_w_in': nrm(18, (N_MLA_LAYERS, D_MODEL, Q_LORA + KV_LORA + QK_ROPE), D_MODEL ** -0.5),
        'mla_q_norm_g': 1.0 + nrm(19, (N_MLA_LAYERS, Q_LORA), 0.02),
        'mla_w_uq': nrm(20, (N_MLA_LAYERS, Q_LORA, MLA_HEADS * (QK_NOPE + QK_ROPE)), Q_LORA ** -0.5),
        'mla_kv_norm_g': 1.0 + nrm(21, (N_MLA_LAYERS, KV_LORA), 0.02),
        'mla_w_ukv': nrm(22, (N_MLA_LAYERS, KV_LORA, MLA_HEADS * (QK_NOPE + V_HEAD)), KV_LORA ** -0.5),
        'mla_w_out': nrm(23, (N_MLA_LAYERS, MLA_HEADS * V_HEAD, D_MODEL), (MLA_HEADS * V_HEAD) ** -0.5),
        'fn_w_out': nrm(24, (N_FOURIER_LAYERS, D_MODEL, D_MODEL), D_MODEL ** -0.5),
        'router_w': nrm(25, (DEPTH, D_MODEL, N_EXPERTS), D_MODEL ** -0.5),
        'router_b': nrm(26, (DEPTH, N_EXPERTS), 0.01),
        'exp_w1': nrm(27, (DEPTH, N_EXPERTS, D_MODEL, 2 * D_EXPERT), D_MODEL ** -0.5),
        'exp_b1': nrm(28, (DEPTH, N_EXPERTS, 2 * D_EXPERT), 0.01),
        'exp_w2': nrm(29, (DEPTH, N_EXPERTS, D_EXPERT, D_MODEL), D_EXPERT ** -0.5),
        'exp_b2': nrm(30, (DEPTH, N_EXPERTS, D_MODEL), 0.01),
        'final_g': 1.0 + nrm(31, (D_MODEL,), 0.02),
    }


def reference(x_prompt, x_sample, cache_mla_ckv, cache_mla_kpe, state_mlstm_C, state_mlstm_n, state_mlstm_m,
              c, c_ctx, norm1_g, norm2_g, ada_w, ada_b, ml_w_in, ml_gate_b, ml_norm_g, ml_w_out,
              mla_w_in, mla_q_norm_g, mla_w_uq, mla_kv_norm_g, mla_w_ukv, mla_w_out, fn_w_out,
              router_w, router_b, exp_w1, exp_b1, exp_w2, exp_b2, final_g):
    p = dict(norm1_g=norm1_g, norm2_g=norm2_g, ada_w=ada_w, ada_b=ada_b,
             ml_w_in=ml_w_in, ml_gate_b=ml_gate_b, ml_norm_g=ml_norm_g, ml_w_out=ml_w_out,
             mla_w_in=mla_w_in, mla_q_norm_g=mla_q_norm_g, mla_w_uq=mla_w_uq,
             mla_kv_norm_g=mla_kv_norm_g, mla_w_ukv=mla_w_ukv, mla_w_out=mla_w_out,
             fn_w_out=fn_w_out, router_w=router_w, router_b=router_b,
             exp_w1=exp_w1, exp_b1=exp_b1, exp_w2=exp_w2, exp_b2=exp_b2, final_g=final_g)
    y_prompt, saved = run_trunk(x_prompt, c_ctx[None, None, :], p, None)
    new_mla_ckv = jnp.stack(saved[0], axis=1)
    new_mla_kpe = jnp.stack(saved[1], axis=1)
    new_mlstm_C = jnp.stack(saved[2], axis=1)
    new_mlstm_n = jnp.stack(saved[3], axis=1)
    new_mlstm_m = jnp.stack(saved[4], axis=1)
    y_sample, _ = run_trunk(x_sample, c[:, None, :], p,
                            (cache_mla_ckv, cache_mla_kpe, state_mlstm_C, state_mlstm_n, state_mlstm_m))
    return (y_prompt, y_sample, new_mla_ckv, new_mla_kpe, new_mlstm_C, new_mlstm_n, new_mlstm_m)
```

```python
import functools

import numpy as np
import jax
import jax.numpy as jnp
from jax import lax
from jax.experimental import pallas as pl
from jax.experimental.pallas import tpu as pltpu

F32 = jnp.float32
BF16 = jnp.bfloat16

D = 1024
DEPTH = 4
N_CTX_SEQ, CTX_T = 16, 256
N_LAT_SEQ, LAT_T = 2, 2048
N_CTX = N_CTX_SEQ * CTX_T
N_TOK = N_CTX + N_LAT_SEQ * LAT_T
TM = 256
N_TILES = N_TOK // TM
CTX_TILES = N_CTX // TM
LAT_TILES = LAT_T // TM
N_SEQ = N_CTX_SEQ + N_LAT_SEQ
PAST_LEN = 512
GRID_W = 64
RMS_EPS = 1e-6

ML_HEADS, ML_DK, ML_DV = 8, 64, 128
ML_AUG = 256
ML_N_LANE, ML_M_LANE = ML_DV, ML_DV + 1

MLA_HEADS, QK_NOPE, QK_ROPE, V_HEAD = 16, 128, 64, 128
Q_LORA, KV_LORA = 384, 256
ROPE_THETA = 10000.0
ATT_HG = 4
ATT_SCALE = (QK_NOPE + QK_ROPE) ** -0.5

FN_GROUPS, FN_GW = 4, 256

N_EXPERTS, TOP_K = 32, 4
SWIGLU_ALPHA, SWIGLU_LIMIT = 1.702, 7.0
MOE_BM = 256
N_ASSIGN = N_TOK * TOP_K
MOE_NB = N_ASSIGN // MOE_BM + N_EXPERTS
Y4_ROWS = N_ASSIGN + 2 * MOE_BM

VMEM_LIMIT = 56 * 1024 * 1024


def _cp(n_grid_axes, vmem=None):
    return pltpu.CompilerParams(dimension_semantics=("arbitrary",) * n_grid_axes, vmem_limit_bytes=vmem)


def _dot(a, b):
    return jnp.dot(a, b, preferred_element_type=F32)


def _dot_nt(a, b):
    return lax.dot_general(a, b, (((1,), (1,)), ((), ())), preferred_element_type=F32)


def _dot_tn(a, b):
    return lax.dot_general(a, b, (((0,), (0,)), ((), ())), preferred_element_type=F32)


def _sigmoid(x):
    return 1.0 / (1.0 + jnp.exp(-x))


def _log_sigmoid(x):
    return jnp.minimum(x, 0.0) - jnp.log(1.0 + jnp.exp(-jnp.abs(x)))


def _rms(x):
    return x * lax.rsqrt(jnp.mean(x * x, axis=-1, keepdims=True) + RMS_EPS)


def _norm_mod(x, g, shift, scale):
    return (_rms(x) * g) * (1.0 + scale) + shift


def _split2(x):
    hi = x.astype(BF16)
    lo = (x - hi.astype(F32)).astype(BF16)
    return hi, lo


def _split3(x):
    hi = x.astype(BF16)
    r = x - hi.astype(F32)
    mid = r.astype(BF16)
    lo = (r - mid.astype(F32)).astype(BF16)
    return hi, mid, lo


def _tile_seq(r):
    return jnp.where(r < CTX_TILES, r, CTX_TILES + (r - CTX_TILES) // LAT_TILES)


def _lat_seq(r):
    return jnp.clip((r - CTX_TILES) // LAT_TILES, 0, N_LAT_SEQ - 1)


MOD_TN = 1536


def _mod_kernel(c_ref, w_ref, b_ref, o_ref):
    a = c_ref[...]
    s_hi, s_lo = _split2(a * _sigmoid(a))
    w_hi, w_lo = _split2(w_ref[0])
    o_ref[0] = _dot_split(s_hi, s_lo, w_hi, w_lo) + b_ref[0]


def _modulation(cond8, ada_w, ada_b):
    n_col = ada_w.shape[-1]
    return pl.pallas_call(
        _mod_kernel,
        out_shape=jax.ShapeDtypeStruct((DEPTH, 8, n_col), F32),
        grid=(DEPTH, n_col // MOD_TN),
        in_specs=[pl.BlockSpec((8, D), lambda l, j: (0, 0)),
                  pl.BlockSpec((1, D, MOD_TN), lambda l, j: (l, 0, j)),
                  pl.BlockSpec((1, 1, MOD_TN), lambda l, j: (l, 0, j))],
        out_specs=pl.BlockSpec((1, 8, MOD_TN), lambda l, j: (l, 0, j)),
        compiler_params=_cp(2, VMEM_LIMIT),
    )(cond8, ada_w, ada_b.reshape(DEPTH, 1, n_col))


def _ml_in_kernel(x_ref, mod_ref, g_ref, w_ref, wg_ref, wgt_ref, qkv_ref, o_ref, gg_ref, gt_ref):
    h = _norm_mod(x_ref[...], g_ref[...], mod_ref[0, 0:1, :], mod_ref[0, 1:2, :]).astype(BF16)
    hk = ML_HEADS * ML_DK
    lane = lax.broadcasted_iota(jnp.int32, (1, 2 * hk), 1)
    qscale = jnp.where(lane < hk, ML_DK ** -0.5, 1.0).astype(F32)
    qkv_ref[:, 0:2 * hk] = (_dot(h, w_ref[:, 0:2 * hk]) * qscale).astype(BF16)
    qkv_ref[:, 2 * hk:2 * hk + D] = _dot(h, w_ref[:, 2 * hk:2 * hk + D]).astype(BF16)
    o_ref[...] = _dot(h, w_ref[:, 2 * hk + D:2 * hk + 2 * D])
    gg_ref[...] = _dot(h, wg_ref[...])
    gt_ref[...] = _dot_nt(wgt_ref[...], h)


def _ml_in(x, modt, norm_g, w_main, w_g, w_gt):
    tile = lambda i: (i, 0)
    const = lambda i: (0, 0)
    return pl.pallas_call(
        _ml_in_kernel,
        out_shape=(jax.ShapeDtypeStruct((N_TOK, 2 * D), BF16),
                   jax.ShapeDtypeStruct((N_TOK, D), F32),
                   jax.ShapeDtypeStruct((N_TOK, 128), F32),
                   jax.ShapeDtypeStruct((32, N_TOK), F32)),
        grid=(N_TILES,),
        in_specs=[pl.BlockSpec((TM, D), tile),
                  pl.BlockSpec((1, 6, D), lambda i: (i, 0, 0)),
                  pl.BlockSpec((1, D), const),
                  pl.BlockSpec((D, 3 * D), const),
                  pl.BlockSpec((D, 128), const),
                  pl.BlockSpec((32, D), const)],
        out_specs=(pl.BlockSpec((TM, 2 * D), tile),
                   pl.BlockSpec((TM, D), tile),
                   pl.BlockSpec((TM, 128), tile),
                   pl.BlockSpec((32, TM), lambda i: (0, i))),
        compiler_params=_cp(1, VMEM_LIMIT),
    )(x, modt, norm_g, w_main, w_g, w_gt)


def _ml_direction(lower, q_ref, k_ref, v_ref, g_ref, gt_ref, gbrow_ref, gbcol_ref, h_ref, st_ref, c_s, m_s):
    L = TM
    d_off = 0 if lower else ML_HEADS
    row = lax.broadcasted_iota(jnp.int32, (L, L), 0)
    col = lax.broadcasted_iota(jnp.int32, (L, L), 1)
    mask = (col <= row) if lower else (col >= row)
    tri = jnp.where(mask, 1.0, 0.0).astype(BF16)

    lane = lax.broadcasted_iota(jnp.int32, (L, 128), 1)
    gc = g_ref[...] + gbrow_ref[...]
    gc = jnp.where((lane >= 2 * ML_HEADS) & (lane < 4 * ML_HEADS), _log_sigmoid(gc), gc)
    bc = sum(_dot(tri, p) for p in _split3(gc))
    sub = lax.broadcasted_iota(jnp.int32, (32, L), 0)
    gr = gt_ref[...] + gbcol_ref[:, 0:1]
    gr = jnp.where(sub >= 2 * ML_HEADS, _log_sigmoid(gr), gr)
    br = sum(_dot_nt(p, tri) for p in _split3(gr))

    q = q_ref[...]
    k = k_ref[...]
    v = v_ref[...]
    lane_a = lax.broadcasted_iota(jnp.int32, (L, ML_AUG - ML_DV), 1)
    ones_blk = jnp.where(lane_a == 0, 1.0, 0.0).astype(BF16)
    lane_s = lax.broadcasted_iota(jnp.int32, (ML_DK, ML_AUG), 1)

    for hd in range(ML_HEADS):
        ci = d_off + hd
        cf = 2 * ML_HEADS + d_off + hd
        i_col, b_col = gc[:, ci:ci + 1], bc[:, cf:cf + 1]
        i_row, b_row = gr[ci:ci + 1, :], br[cf:cf + 1, :]
        total = b_col[L - 1:L, :] if lower else b_col[0:1, :]
        m = m_s[hd][0:1, 0:1]
        c_aug = c_s[hd]
        dmat = jnp.where(mask, (b_col - b_row) + i_row, -jnp.inf)
        mi = jnp.max(dmat, axis=1, keepdims=True)
        a_col = b_col + m
        m_row = jnp.maximum(a_col, mi)
        qh = q[:, hd * ML_DK:(hd + 1) * ML_DK]
        kh = k[:, hd * ML_DK:(hd + 1) * ML_DK]
        v_aug = jnp.concatenate([v[:, hd * ML_DV:(hd + 1) * ML_DV], ones_blk], axis=1)
        p = (_dot_nt(qh, kh) * jnp.exp(dmat - mi)).astype(BF16)
        num = jnp.exp(mi - m_row) * _dot(p, v_aug) + jnp.exp(a_col - m_row) * _dot(qh, c_aug.astype(BF16))
        den = jnp.maximum(jnp.abs(num[:, ML_N_LANE:ML_N_LANE + 1]), jnp.exp(-m_row))
        h_ref[:, hd * ML_DV:(hd + 1) * ML_DV] = num[:, 0:ML_DV] / den

        g_col = (total - b_col) + i_col
        m_new = jnp.maximum(total + m, jnp.max(g_col, axis=0, keepdims=True))
        kw = (kh.astype(F32) * jnp.exp(g_col - m_new)).astype(BF16)
        c_new = jnp.exp(total + m - m_new) * c_aug + _dot_tn(kw, v_aug)
        c_s[hd] = c_new
        m_s[hd] = jnp.broadcast_to(m_new, (8, 128))
        st_ref[0, hd] = jnp.where(lane_s == ML_M_LANE, m_new, c_new)


def _ml_core_kernel(qf, kf, vf, gf, gtf, qb, kb, vb, gb, gtb, gbrow, gbcol, initf, initb,
                    hf_ref, hb_ref, stf_ref, stb_ref, cf_s, cb_s, mf_s, mb_s):
    i = pl.program_id(0)
    start_f = (i <= CTX_TILES) | (i == CTX_TILES + LAT_TILES)
    start_b = (i == 0) | (i == LAT_TILES) | (i >= 2 * LAT_TILES)

    def load_state(init_ref, c_s, m_s):
        c_s[...] = init_ref[0]
        for hd in range(ML_HEADS):
            m_s[hd] = jnp.broadcast_to(init_ref[0, hd][0:1, ML_M_LANE:ML_M_LANE + 1], (8, 128))

    @pl.when(start_f)
    def _():
        load_state(initf, cf_s, mf_s)

    @pl.when(start_b)
    def _():
        load_state(initb, cb_s, mb_s)

    _ml_direction(True, qf, kf, vf, gf, gtf, gbrow, gbcol, hf_ref, stf_ref, cf_s, mf_s)
    _ml_direction(False, qb, kb, vb, gb, gtb, gbrow, gbcol, hb_ref, stb_ref, cb_s, mb_s)


def _ml_core(qkv, gg, gt, gb_row, gb_col, init_f, init_b):
    last = N_TILES - 1
    fwd = lambda i: i
    bwd = lambda i: last - i
    hk = ML_HEADS * ML_DK

    def specs(t):
        return [pl.BlockSpec((TM, hk), lambda i: (t(i), 0)),
                pl.BlockSpec((TM, hk), lambda i: (t(i), 1)),
                pl.BlockSpec((TM, D), lambda i: (t(i), 1)),
                pl.BlockSpec((TM, 128), lambda i: (t(i), 0)),
                pl.BlockSpec((32, TM), lambda i: (0, t(i)))]

    def st_spec(t):
        return pl.BlockSpec((1, ML_HEADS, ML_DK, ML_AUG), lambda i: (_tile_seq(t(i)), 0, 0, 0))

    st_shape = jax.ShapeDtypeStruct((N_SEQ, ML_HEADS, ML_DK, ML_AUG), F32)
    return pl.pallas_call(
        _ml_core_kernel,
        out_shape=(jax.ShapeDtypeStruct((N_TOK, D), F32), jax.ShapeDtypeStruct((N_TOK, D), F32),
                   st_shape, st_shape),
        grid=(N_TILES,),
        in_specs=specs(fwd) + specs(bwd) + [
            pl.BlockSpec((1, 128), lambda i: (0, 0)),
            pl.BlockSpec((32, 128), lambda i: (0, 0)),
            st_spec(fwd), st_spec(bwd)],
        out_specs=(pl.BlockSpec((TM, D), lambda i: (i, 0)),
                   pl.BlockSpec((TM, D), lambda i: (last - i, 0)),
                   st_spec(fwd), st_spec(bwd)),
        scratch_shapes=[pltpu.VMEM((ML_HEADS, ML_DK, ML_AUG), F32),
                        pltpu.VMEM((ML_HEADS, ML_DK, ML_AUG), F32),
                        pltpu.VMEM((ML_HEADS, 8, 128), F32),
                        pltpu.VMEM((ML_HEADS, 8, 128), F32)],
        compiler_params=_cp(1, VMEM_LIMIT),
    )(qkv, qkv, qkv, gg, gt, qkv, qkv, qkv, gg, gt, gb_row, gb_col, init_f, init_b)


def _ml_out_kernel(hf_ref, hb_ref, o_ref, ng_ref, w_ref, x_ref, mod_ref, out_ref, z_s):
    hh = hf_ref[...] + hb_ref[...]
    o = o_ref[...]
    for hd in range(ML_HEADS):
        sl = slice(hd * ML_DV, (hd + 1) * ML_DV)
        z_s[:, sl] = (_sigmoid(o[:, sl]) * (_rms(hh[:, sl]) * ng_ref[:, sl])).astype(BF16)
    out_ref[...] = x_ref[...] + mod_ref[0, 2:3, :] * _dot(z_s[...], w_ref[...])


def _ml_out(hf, hb, o, norm_g, w_out, x, modt):
    tile = lambda i: (i, 0)
    const = lambda i: (0, 0)
    return pl.pallas_call(
        _ml_out_kernel,
        out_shape=jax.ShapeDtypeStruct((N_TOK, D), F32),
        grid=(N_TILES,),
        in_specs=[pl.BlockSpec((TM, D), tile), pl.BlockSpec((TM, D), tile), pl.BlockSpec((TM, D), tile),
                  pl.BlockSpec((1, D), const), pl.BlockSpec((D, D), const),
                  pl.BlockSpec((TM, D), tile), pl.BlockSpec((1, 6, D), lambda i: (i, 0, 0))],
        out_specs=pl.BlockSpec((TM, D), tile),
        scratch_shapes=[pltpu.VMEM((TM, D), BF16)],
        compiler_params=_cp(1, VMEM_LIMIT),
    )(hf, hb, o, norm_g, w_out, x, modt)


MLA_IN_COLS = 896
NOPE_ALL = MLA_HEADS * QK_NOPE
ROPE_ALL = MLA_HEADS * QK_ROPE


def _mla_in_kernel(x_ref, mod_ref, g_ref, w_ref, qg_ref, kvg_ref, wuq_ref, cos_ref, sin_ref,
                   qn_ref, qr_ref, ckv_ref, kpe_ref):
    h = _norm_mod(x_ref[...], g_ref[...], mod_ref[0, 0:1, :], mod_ref[0, 1:2, :]).astype(BF16)
    c = _dot(h, w_ref[...])
    cos, sin = cos_ref[...], sin_ref[...]
    ckv_ref[...] = _rms(c[:, Q_LORA:Q_LORA + KV_LORA]) * kvg_ref[...]
    kpe_ref[...] = c[:, 640:704] * cos[:, 0:QK_ROPE] + c[:, 768:832] * sin[:, 0:QK_ROPE]
    cq = (_rms(c[:, 0:Q_LORA]) * qg_ref[...]).astype(BF16)
    qn_ref[...] = _dot(cq, wuq_ref[:, 0:NOPE_ALL]).astype(BF16)
    rope = _dot(cq, wuq_ref[:, NOPE_ALL:NOPE_ALL + ROPE_ALL])
    swapped = _dot(cq, wuq_ref[:, NOPE_ALL + ROPE_ALL:NOPE_ALL + 2 * ROPE_ALL])
    for s in range(ROPE_ALL // 128):
        sl = slice(s * 128, (s + 1) * 128)
        qr_ref[:, sl] = (rope[:, sl] * cos + swapped[:, sl] * sin).astype(BF16)


def _mla_in(x, modt, norm_g, w_in2, qg, kvg, w_uq2, cos2, sin2):
    tile = lambda i: (i, 0)
    const = lambda i: (0, 0)
    return pl.pallas_call(
        _mla_in_kernel,
        out_shape=(jax.ShapeDtypeStruct((N_TOK, NOPE_ALL), BF16),
                   jax.ShapeDtypeStruct((N_TOK, ROPE_ALL), BF16),
                   jax.ShapeDtypeStruct((N_TOK, KV_LORA), F32),
                   jax.ShapeDtypeStruct((N_TOK, QK_ROPE), F32)),
        grid=(N_TILES,),
        in_specs=[pl.BlockSpec((TM, D), tile), pl.BlockSpec((1, 6, D), lambda i: (i, 0, 0)),
                  pl.BlockSpec((1, D), const), pl.BlockSpec((D, MLA_IN_COLS), const),
                  pl.BlockSpec((1, Q_LORA), const), pl.BlockSpec((1, KV_LORA), const),
                  pl.BlockSpec((Q_LORA, NOPE_ALL + 2 * ROPE_ALL), const),
                  pl.BlockSpec((TM, 128), tile), pl.BlockSpec((TM, 128), tile)],
        out_specs=(pl.BlockSpec((TM, NOPE_ALL), tile), pl.BlockSpec((TM, ROPE_ALL), tile),
                   pl.BlockSpec((TM, KV_LORA), tile), pl.BlockSpec((TM, QK_ROPE), tile)),
        compiler_params=_cp(1, VMEM_LIMIT),
    )(x, modt, norm_g, w_in2, qg, kvg, w_uq2, cos2, sin2)


def _kv_expand_kernel(c_ref, w_ref, kn_ref, v_ref):
    c = c_ref[...].astype(BF16)
    kn_ref[...] = _dot(c, w_ref[:, 0:NOPE_ALL]).astype(BF16)
    v_ref[...] = _dot(c, w_ref[:, NOPE_ALL:2 * NOPE_ALL]).astype(BF16)


def _kv_expand(ckv, w_ukv2):
    rows = ckv.shape[0]
    tile = lambda i: (i, 0)
    shp = jax.ShapeDtypeStruct((rows, NOPE_ALL), BF16)
    return pl.pallas_call(
        _kv_expand_kernel,
        out_shape=(shp, shp),
        grid=(rows // TM,),
        in_specs=[pl.BlockSpec((TM, KV_LORA), tile), pl.BlockSpec((KV_LORA, 2 * NOPE_ALL), lambda i: (0, 0))],
        out_specs=(pl.BlockSpec((TM, NOPE_ALL), tile), pl.BlockSpec((TM, NOPE_ALL), tile)),
        compiler_params=_cp(1, VMEM_LIMIT),
    )(ckv, w_ukv2)


def _attend(qn_ref, qr_ref, segs, out_ref):
    kr = [s[2][...].astype(BF16) for s in segs]
    for hd in range(ATT_HG):
        sl = slice(hd * QK_NOPE, (hd + 1) * QK_NOPE)
        qn = qn_ref[:, sl]
        qr = qr_ref[:, hd * QK_ROPE:(hd + 1) * QK_ROPE]
        scores = [(_dot_nt(qn, s[0][:, sl]) + _dot_nt(qr, kr_s)) * ATT_SCALE for s, kr_s in zip(segs, kr)]
        m = functools.reduce(jnp.maximum, [jnp.max(s, axis=1, keepdims=True) for s in scores])
        ps = [jnp.exp(s - m) for s in scores]
        denom = sum(jnp.sum(p, axis=1, keepdims=True) for p in ps)
        acc = sum(_dot(p.astype(BF16), s[1][:, sl]) for p, s in zip(ps, segs))
        out_ref[:, sl] = (acc / denom).astype(BF16)


def _attn_kernel(qn_ref, qr_ref, knc_ref, vc_ref, krc_ref, knl_ref, vl_ref, krl_ref, knp_ref, vp_ref, krp_ref,
                 out_ref):
    r = pl.program_id(1)

    @pl.when(r < CTX_TILES)
    def _():
        _attend(qn_ref, qr_ref, [(knc_ref, vc_ref, krc_ref)], out_ref)

    @pl.when(r >= CTX_TILES)
    def _():
        _attend(qn_ref, qr_ref, [(knp_ref, vp_ref, krp_ref), (knl_ref, vl_ref, krl_ref)], out_ref)


def _attention(qn, qr, kn_tok, v_tok, kr_tok, kn_past, v_past, kr_past):
    hw = ATT_HG * QK_NOPE
    ctx_t = lambda g, r: (jnp.minimum(r, CTX_TILES - 1), g)
    lat_t = lambda g, r: (N_CTX // LAT_T + _lat_seq(r), g)
    past_t = lambda g, r: (_lat_seq(r), g)
    col0 = lambda f: (lambda g, r: (f(g, r)[0], 0))
    return pl.pallas_call(
        _attn_kernel,
        out_shape=jax.ShapeDtypeStruct((N_TOK, NOPE_ALL), BF16),
        grid=(MLA_HEADS // ATT_HG, N_TILES),
        in_specs=[pl.BlockSpec((TM, hw), lambda g, r: (r, g)),
                  pl.BlockSpec((TM, ATT_HG * QK_ROPE), lambda g, r: (r, g)),
                  pl.BlockSpec((CTX_T, hw), ctx_t), pl.BlockSpec((CTX_T, hw), ctx_t),
                  pl.BlockSpec((CTX_T, QK_ROPE), col0(ctx_t)),
                  pl.BlockSpec((LAT_T, hw), lat_t), pl.BlockSpec((LAT_T, hw), lat_t),
                  pl.BlockSpec((LAT_T, QK_ROPE), col0(lat_t)),
                  pl.BlockSpec((PAST_LEN, hw), past_t), pl.BlockSpec((PAST_LEN, hw), past_t),
                  pl.BlockSpec((PAST_LEN, QK_ROPE), col0(past_t))],
        out_specs=pl.BlockSpec((TM, hw), lambda g, r: (r, g)),
        compiler_params=_cp(2, VMEM_LIMIT),
    )(qn, qr, kn_tok, v_tok, kr_tok, kn_tok, v_tok, kr_tok, kn_past, v_past, kr_past)


def _res_linear_kernel(a_ref, w_ref, x_ref, mod_ref, out_ref):
    out_ref[...] = x_ref[...] + mod_ref[0, 2:3, :] * _dot(a_ref[...], w_ref[...])


def _res_linear(a, w, x, modt):
    k = a.shape[1]
    tile = lambda i: (i, 0)
    return pl.pallas_call(
        _res_linear_kernel,
        out_shape=jax.ShapeDtypeStruct((N_TOK, D), F32),
        grid=(N_TILES,),
        in_specs=[pl.BlockSpec((TM, k), tile), pl.BlockSpec((k, D), lambda i: (0, 0)),
                  pl.BlockSpec((TM, D), tile), pl.BlockSpec((1, 6, D), lambda i: (i, 0, 0))],
        out_specs=pl.BlockSpec((TM, D), tile),
        compiler_params=_cp(1, VMEM_LIMIT),
    )(a, w, x, modt)


def _dot_split(a_hi, a_lo, b_hi, b_lo):
    return _dot(a_hi, b_hi) + (_dot(a_hi, b_lo) + _dot(a_lo, b_hi))


def _fn_channel_dft(x, g, shift, scale, wc_hi, wc_lo):
    h_hi, h_lo = _split2(_norm_mod(x, g, shift, scale))
    a_parts, b_parts = [], []
    for grp in range(FN_GROUPS):
        sl = slice(grp * FN_GW, (grp + 1) * FN_GW)
        ab = _dot_split(h_hi[:, sl], h_lo[:, sl], wc_hi, wc_lo)
        a_parts.append(ab[:, 0:FN_GW])
        b_parts.append(ab[:, FN_GW:2 * FN_GW])
    return jnp.concatenate(a_parts, axis=1), jnp.concatenate(b_parts, axis=1)


FN_STEPS = CTX_TILES + N_LAT_SEQ * 2 * LAT_TILES


def _fn_step(t):
    u = jnp.maximum(t - CTX_TILES, 0)
    seq, ph, tile = u // (2 * LAT_TILES), (u % (2 * LAT_TILES)) // LAT_TILES, u % LAT_TILES
    is_ctx = t < CTX_TILES
    return is_ctx, ph, tile, jnp.where(is_ctx, t, CTX_TILES + seq * LAT_TILES + tile)


def _fn_kernel(x_ref, mod_ref, g_ref, wc_hi_ref, wc_lo_ref, tc_hi_ref, tc_lo_ref,
               tl_hi_ref, tl_lo_ref, w_ref, out_ref, ab_hi_s, ab_lo_s):
    is_ctx, ph, tile, _ = _fn_step(pl.program_id(0))
    g = g_ref[...]
    shift, scale, gate = mod_ref[0, 0:1, :], mod_ref[0, 1:2, :], mod_ref[0, 2:3, :]
    wc_hi, wc_lo = wc_hi_ref[...], wc_lo_ref[...]

    def finish(f, t_len):
        f = f * ((t_len * FN_GW) ** -0.5)
        out_ref[...] = x_ref[...] + gate * _dot(f.astype(BF16), w_ref[...])

    @pl.when(is_ctx)
    def _():
        a, b = _fn_channel_dft(x_ref[...], g, shift, scale, wc_hi, wc_lo)
        ab_hi, ab_lo = _split2(jnp.concatenate([a, b], axis=0))
        finish(_dot_split(tc_hi_ref[...], tc_lo_ref[...], ab_hi, ab_lo), CTX_T)

    @pl.when(jnp.logical_not(is_ctx) & (ph == 0))
    def _():
        a, b = _fn_channel_dft(x_ref[...], g, shift, scale, wc_hi, wc_lo)
        a_hi, a_lo = _split2(a)
        b_hi, b_lo = _split2(b)
        rows = pl.ds(pl.multiple_of(tile * TM, TM), TM)
        rows_b = pl.ds(pl.multiple_of(LAT_T + tile * TM, TM), TM)
        ab_hi_s[rows, :] = a_hi
        ab_lo_s[rows, :] = a_lo
        ab_hi_s[rows_b, :] = b_hi
        ab_lo_s[rows_b, :] = b_lo

    @pl.when(jnp.logical_not(is_ctx) & (ph == 1))
    def _():
        finish(_dot_split(tl_hi_ref[...], tl_lo_ref[...], ab_hi_s[...], ab_lo_s[...]), LAT_T)


def _fourier(x, modt, norm_g, tabs, w_out):
    wc_hi, wc_lo, tc_hi, tc_lo, tl_hi, tl_lo = tabs
    const = lambda t: (0, 0)
    x_tile = lambda t: (_fn_step(t)[3], 0)

    def out_tile(t):
        is_ctx, ph, tile, gt = _fn_step(t)
        return (jnp.where(is_ctx | (ph == 1), gt, gt - tile), 0)

    def tab_row(t):
        is_ctx, ph, tile, _ = _fn_step(t)
        return (jnp.where(is_ctx | (ph == 0), 0, tile), 0)

    return pl.pallas_call(
        _fn_kernel,
        out_shape=jax.ShapeDtypeStruct((N_TOK, D), F32),
        grid=(FN_STEPS,),
        in_specs=[pl.BlockSpec((TM, D), x_tile),
                  pl.BlockSpec((1, 6, D), lambda t: (_fn_step(t)[3], 0, 0)),
                  pl.BlockSpec((1, D), const),
                  pl.BlockSpec((FN_GW, 2 * FN_GW), const), pl.BlockSpec((FN_GW, 2 * FN_GW), const),
                  pl.BlockSpec((CTX_T, 2 * CTX_T), const), pl.BlockSpec((CTX_T, 2 * CTX_T), const),
                  pl.BlockSpec((TM, 2 * LAT_T), tab_row), pl.BlockSpec((TM, 2 * LAT_T), tab_row),
                  pl.BlockSpec((D, D), const)],
        out_specs=pl.BlockSpec((TM, D), out_tile),
        scratch_shapes=[pltpu.VMEM((2 * LAT_T, D), BF16), pltpu.VMEM((2 * LAT_T, D), BF16)],
        compiler_params=_cp(1, VMEM_LIMIT),
    )(x, modt, norm_g, wc_hi, wc_lo, tc_hi, tc_lo, tl_hi, tl_lo, w_out)


def _dft_tables():
    def cos_sin(n):
        k = jnp.arange(n, dtype=jnp.int32)
        ang = ((k[:, None] * k[None, :]) % n).astype(F32) * (2.0 * np.pi / n)
        return jnp.cos(ang), jnp.sin(ang)

    cc, sc = cos_sin(FN_GW)
    wc = jnp.concatenate([cc, sc], axis=1)
    out = list(_split2(wc))
    for t_len in (CTX_T, LAT_T):
        ct, st = cos_sin(t_len)
        out += list(_split2(jnp.concatenate([ct, -st], axis=1)))
    return tuple(out)


def _router_kernel(x_ref, mod_ref, g_ref, rw_hi_ref, rw_lo_ref, rb_ref, h_ref, idx_ref, gate_ref):
    h = _norm_mod(x_ref[...], g_ref[...], mod_ref[0, 3:4, :], mod_ref[0, 4:5, :])
    h_ref[...] = h
    h_hi, h_lo = _split2(h)
    logits = _dot_split(h_hi, h_lo, rw_hi_ref[...], rw_lo_ref[...]) + rb_ref[...]
    lane = lax.broadcasted_iota(jnp.int32, logits.shape, 1)
    lane_f = lane.astype(F32)
    cur = jnp.where(lane < N_EXPERTS, logits, -jnp.inf)
    vals, idxs = [], []
    for _ in range(TOP_K):
        m = jnp.max(cur, axis=1, keepdims=True)
        ik = jnp.min(jnp.where(cur == m, lane_f, 128.0), axis=1, keepdims=True).astype(jnp.int32)
        vals.append(m)
        idxs.append(ik)
        cur = jnp.where(lane == ik, -jnp.inf, cur)
    es = [jnp.exp(v - vals[0]) for v in vals]
    denom = functools.reduce(lambda a, b: a + b, es)
    idx_out = jnp.zeros(logits.shape, jnp.int32)
    gate_out = jnp.zeros(logits.shape, F32)
    for kk in range(TOP_K):
        idx_out = jnp.where(lane == kk, idxs[kk], idx_out)
        gate_out = jnp.where(lane == kk, es[kk] / denom, gate_out)
    idx_ref[...] = idx_out
    gate_ref[...] = gate_out


def _router(x, modt, norm_g, rw_hi, rw_lo, rb):
    tile = lambda i: (i, 0)
    const = lambda i: (0, 0)
    return pl.pallas_call(
        _router_kernel,
        out_shape=(jax.ShapeDtypeStruct((N_TOK, D), F32),
                   jax.ShapeDtypeStruct((N_TOK, 128), jnp.int32),
                   jax.ShapeDtypeStruct((N_TOK, 128), F32)),
        grid=(N_TILES,),
        in_specs=[pl.BlockSpec((TM, D), tile), pl.BlockSpec((1, 6, D), lambda i: (i, 0, 0)),
                  pl.BlockSpec((1, D), const), pl.BlockSpec((D, 128), const), pl.BlockSpec((D, 128), const),
                  pl.BlockSpec((1, 128), const)],
        out_specs=(pl.BlockSpec((TM, D), tile), pl.BlockSpec((TM, 128), tile), pl.BlockSpec((TM, 128), tile)),
        compiler_params=_cp(1, VMEM_LIMIT),
    )(x, modt, norm_g, rw_hi, rw_lo, rb)


def _route_plan(idx):
    ids = idx[:, 0:TOP_K].reshape(-1)
    order = jnp.argsort(ids, stable=True).astype(jnp.int32)
    counts = jnp.sum((ids[:, None] == jnp.arange(N_EXPERTS, dtype=jnp.int32)[None, :]).astype(jnp.int32), axis=0)
    starts = jnp.cumsum(counts) - counts
    nblk = (counts + MOE_BM - 1) // MOE_BM
    pb_end = jnp.cumsum(nblk)
    pb_start = pb_end - nblk
    total = pb_end[-1]
    b = jnp.arange(MOE_NB, dtype=jnp.int32)
    be = jnp.minimum(jnp.sum((pb_end[None, :] <= b[:, None]).astype(jnp.int32), axis=1), N_EXPERTS - 1)
    be = jnp.where(b < total, be, be[jnp.maximum(total - 1, 0)])
    local = b - pb_start[be]
    base = starts[be] + local * MOE_BM
    nvalid = jnp.clip(counts[be] - local * MOE_BM, 0, MOE_BM)
    first = ((local == 0) & (b < total)).astype(jnp.int32)
    return (be.astype(jnp.int32), first, base.astype(jnp.int32), nvalid.astype(jnp.int32),
            total.reshape(1).astype(jnp.int32), order)


def _expert_kernel(be_ref, first_ref, base_ref, nv_ref, total_ref, order_ref,
                   x_hbm, w1_ref, b1_ref, w2_ref, b2_ref, y4_hbm,
                   xs_buf, y_buf, w1b, w2b, gsem, ssem):
    b = pl.program_id(0)
    total = total_ref[0]

    def gather_copy(tok, slot, r):
        return pltpu.make_async_copy(x_hbm.at[pl.ds(tok, 1), :], xs_buf.at[slot, pl.ds(r, 1), :], gsem.at[slot])

    def scatter_copy(slot, r, dst):
        return pltpu.make_async_copy(y_buf.at[slot, pl.ds(r, 1), :], y4_hbm.at[pl.ds(dst, 1), :], ssem.at[slot])

    def wait_gather(slot):
        pltpu.make_async_copy(x_hbm.at[pl.ds(0, MOE_BM), :], xs_buf.at[slot], gsem.at[slot]).wait()

    def wait_scatter(slot):
        pltpu.make_async_copy(y_buf.at[slot], y4_hbm.at[pl.ds(0, MOE_BM), :], ssem.at[slot]).wait()

    def issue_gather(blk):
        slot = blk % 2
        base, nv = base_ref[blk], nv_ref[blk]

        def body(r, carry):
            a = order_ref[base + jnp.minimum(r, nv - 1)]
            gather_copy(a >> 2, slot, r).start()
            return carry
        lax.fori_loop(0, MOE_BM, body, 0, unroll=8)

    @pl.when(b == 0)
    def _():
        y_buf[...] = jnp.zeros(y_buf.shape, F32)
        for s in range(2):
            cp = pltpu.make_async_copy(y_buf.at[s], y4_hbm.at[pl.ds(N_ASSIGN + s * MOE_BM, MOE_BM), :], ssem.at[s])
            cp.start()
            cp.wait()
        issue_gather(0)

    @pl.when(b + 1 < total)
    def _():
        issue_gather(b + 1)

    @pl.when(b < total)
    def _():
        slot = b % 2

        @pl.when(first_ref[b] == 1)
        def _():
            w1b[...] = w1_ref[0, 0].astype(BF16)
            w2b[...] = w2_ref[0, 0].astype(BF16)

        wait_gather(slot)

        @pl.when(b >= 2)
        def _():
            wait_scatter(slot)

        xs = xs_buf[slot].astype(BF16)
        h = _dot(xs, w1b[...]) + b1_ref[0, 0]
        gate = jnp.minimum(h[:, 0:D], SWIGLU_LIMIT)
        up = jnp.clip(h[:, D:2 * D], -SWIGLU_LIMIT, SWIGLU_LIMIT)
        act = (up + 1.0) * (gate * _sigmoid(SWIGLU_ALPHA * gate))
        y_buf[slot] = _dot(act.astype(BF16), w2b[...]) + b2_ref[0, 0]

        base, nv = base_ref[b], nv_ref[b]

        def body(r, carry):
            a = order_ref[base + jnp.minimum(r, nv - 1)]
            dst = jnp.where(r < nv, (a & (TOP_K - 1)) * N_TOK + (a >> 2), N_ASSIGN + slot * MOE_BM + r)
            scatter_copy(slot, r, dst).start()
            return carry
        lax.fori_loop(0, MOE_BM, body, 0, unroll=8)

    @pl.when(b == MOE_NB - 1)
    def _():
        @pl.when(total >= 2)
        def _():
            wait_scatter(total % 2)

        wait_scatter((total - 1) % 2)


def _experts(layer, plan, h2, w1, b1, w2, b2):
    be, first, base, nvalid, total, order = plan
    wmap = lambda b, be, *_: (layer, be[b], 0, 0)
    grid_spec = pltpu.PrefetchScalarGridSpec(
        num_scalar_prefetch=6,
        grid=(MOE_NB,),
        in_specs=[pl.BlockSpec(memory_space=pl.ANY),
                  pl.BlockSpec((1, 1, D, 2 * D), wmap), pl.BlockSpec((1, 1, 1, 2 * D), wmap),
                  pl.BlockSpec((1, 1, D, D), wmap), pl.BlockSpec((1, 1, 1, D), wmap)],
        out_specs=pl.BlockSpec(memory_space=pl.ANY),
        scratch_shapes=[pltpu.VMEM((2, MOE_BM, D), F32), pltpu.VMEM((2, MOE_BM, D), F32),
                        pltpu.VMEM((D, 2 * D), BF16), pltpu.VMEM((D, D), BF16),
                        pltpu.SemaphoreType.DMA((2,)), pltpu.SemaphoreType.DMA((2,))],
    )
    return pl.pallas_call(
        _expert_kernel,
        out_shape=jax.ShapeDtypeStruct((Y4_ROWS, D), F32),
        grid_spec=grid_spec,
        compiler_params=_cp(1, VMEM_LIMIT),
    )(be, first, base, nvalid, total, order, h2, w1, b1.reshape(DEPTH, N_EXPERTS, 1, 2 * D), w2,
      b2.reshape(DEPTH, N_EXPERTS, 1, D))


def _combine_kernel(final, x_ref, y0, y1, y2, y3, gate_ref, mod_ref, fg_ref, out_ref):
    gates = gate_ref[...]
    moe = gates[:, 0:1] * y0[...]
    for kk, y in ((1, y1), (2, y2), (3, y3)):
        moe = moe + gates[:, kk:kk + 1] * y[...]
    x = x_ref[...] + mod_ref[0, 5:6, :] * moe
    out_ref[...] = _rms(x) * fg_ref[...] if final else x


def _combine(x, y4, gate, modt, final_g, final):
    tile = lambda i: (i, 0)
    plane = lambda kk: pl.BlockSpec((TM, D), lambda i: (kk * N_TILES + i, 0))
    return pl.pallas_call(
        functools.partial(_combine_kernel, final),
        out_shape=jax.ShapeDtypeStruct((N_TOK, D), F32),
        grid=(N_TILES,),
        in_specs=[pl.BlockSpec((TM, D), tile), plane(0), plane(1), plane(2), plane(3),
                  pl.BlockSpec((TM, 128), tile), pl.BlockSpec((1, 6, D), lambda i: (i, 0, 0)),
                  pl.BlockSpec((1, D), lambda i: (0, 0))],
        out_specs=pl.BlockSpec((TM, D), tile),
        compiler_params=_cp(1, VMEM_LIMIT),
    )(x, y4, y4, y4, y4, gate, modt, final_g)


def _moe(layer, x, modt, norm2_g, router_w, router_b, w1, b1, w2, b2, final_g, final):
    rw = jnp.pad(router_w, ((0, 0), (0, 128 - N_EXPERTS)))
    rw_hi, rw_lo = _split2(rw)
    rb = jnp.pad(router_b, (0, 128 - N_EXPERTS)).reshape(1, 128)
    h2, idx, gate = _router(x, modt, norm2_g, rw_hi, rw_lo, rb)
    y4 = _experts(layer, _route_plan(idx), h2, w1, b1, w2, b2)
    return _combine(x, y4, gate, modt, final_g, final)


def _rope_tables():
    def tables(n_tokens):
        rows = n_tokens // GRID_W
        row = jnp.repeat(jnp.arange(rows, dtype=F32), GRID_W)
        col = jnp.tile(jnp.arange(GRID_W, dtype=F32), rows)
        n_freq = QK_ROPE // 4
        inv = ROPE_THETA ** (-jnp.arange(n_freq, dtype=F32) / n_freq)
        ang = jnp.stack([row[:, None] * inv, col[:, None] * inv], axis=1)
        return jnp.cos(ang), jnp.sin(ang)

    cos, sin = tables(LAT_T)
    cos64 = jnp.concatenate([cos[:, 0], cos[:, 0], cos[:, 1], cos[:, 1]], axis=1)
    sin64 = jnp.concatenate([-sin[:, 0], sin[:, 0], -sin[:, 1], sin[:, 1]], axis=1)
    cos_all = jnp.concatenate([jnp.ones((N_CTX, QK_ROPE), F32)] + [cos64] * N_LAT_SEQ, axis=0)
    sin_all = jnp.concatenate([jnp.zeros((N_CTX, QK_ROPE), F32)] + [sin64] * N_LAT_SEQ, axis=0)
    return jnp.tile(cos_all, (1, 2)), jnp.tile(sin_all, (1, 2))


_PAIR_SWAP = np.concatenate([np.arange(16, 32), np.arange(0, 16), np.arange(48, 64), np.arange(32, 48)])


def _mla_weights(w_in, w_uq, w_ukv):
    kpe = w_in[:, Q_LORA + KV_LORA:]
    pad = jnp.zeros((D, 64), F32)
    w_in2 = jnp.concatenate([w_in[:, :Q_LORA + KV_LORA], kpe, pad, kpe[:, _PAIR_SWAP], pad], axis=1)
    uq = w_uq.reshape(Q_LORA, MLA_HEADS, QK_NOPE + QK_ROPE)
    rope = uq[:, :, QK_NOPE:]
    w_uq2 = jnp.concatenate([uq[:, :, :QK_NOPE].reshape(Q_LORA, NOPE_ALL), rope.reshape(Q_LORA, ROPE_ALL),
                             rope[:, :, _PAIR_SWAP].reshape(Q_LORA, ROPE_ALL)], axis=1)
    ukv = w_ukv.reshape(KV_LORA, MLA_HEADS, QK_NOPE + V_HEAD)
    w_ukv2 = jnp.concatenate([ukv[:, :, :QK_NOPE].reshape(KV_LORA, NOPE_ALL),
                              ukv[:, :, QK_NOPE:].reshape(KV_LORA, NOPE_ALL)], axis=1)
    return w_in2.astype(BF16), w_uq2.astype(BF16), w_ukv2.astype(BF16)


def _ml_init_state(state_c, state_n, state_m, j, direction):
    c = state_c[:, j, direction].astype(F32)
    n = state_n[:, j, direction].astype(F32)[..., None]
    m = jnp.broadcast_to(state_m[:, j, direction].astype(F32)[..., None, None], n.shape)
    pad = jnp.zeros(c.shape[:-1] + (ML_AUG - ML_DV - 2,), F32)
    lat = jnp.concatenate([c, n, m, pad], axis=-1)
    return jnp.concatenate([jnp.zeros((N_CTX_SEQ,) + lat.shape[1:], F32), lat], axis=0)


def kernel(x_prompt, x_sample, cache_mla_ckv, cache_mla_kpe, state_mlstm_C, state_mlstm_n, state_mlstm_m, c, c_ctx, norm1_g, norm2_g, ada_w, ada_b, ml_w_in, ml_gate_b, ml_norm_g, ml_w_out, mla_w_in, mla_q_norm_g, mla_w_uq, mla_kv_norm_g, mla_w_ukv, mla_w_out, fn_w_out, router_w, router_b, exp_w1, exp_b1, exp_w2, exp_b2, final_g):
    x = jnp.concatenate([x_prompt.reshape(N_CTX, D), x_sample.reshape(N_LAT_SEQ * LAT_T, D)], axis=0)

    cond8 = jnp.concatenate([c_ctx[None, :], c, jnp.zeros((8 - 1 - N_LAT_SEQ, D), F32)], axis=0)
    mod = _modulation(cond8, ada_w, ada_b)
    tile_cond = np.concatenate([np.zeros(CTX_TILES, np.int32)] +
                               [np.full(LAT_TILES, 1 + s, np.int32) for s in range(N_LAT_SEQ)])
    modt = mod[:, tile_cond].reshape(DEPTH, N_TILES, 6, D)

    hk = ML_HEADS * ML_DK
    states = []
    new_ckv = new_kpe = None
    for l in range(DEPTH):
        kind, j = l % 3, l // 3
        n1 = norm1_g[l].reshape(1, D)
        if kind == 0:
            w = ml_w_in[j]
            w_gates = w[:, 2 * hk + 2 * D:]
            qkv, o, gg, gt = _ml_in(x, modt[l], n1, w[:, :2 * hk + 2 * D].astype(BF16),
                                    jnp.pad(w_gates, ((0, 0), (0, 128 - 4 * ML_HEADS))).astype(BF16),
                                    w_gates.T.astype(BF16))
            gb = ml_gate_b[j].reshape(4 * ML_HEADS).astype(F32)
            gb_row = jnp.pad(gb, (0, 128 - 4 * ML_HEADS)).reshape(1, 128)
            gb_col = jnp.broadcast_to(gb[:, None], (4 * ML_HEADS, 128))
            hf, hb, st_f, st_b = _ml_core(qkv, gg, gt, gb_row, gb_col,
                                          _ml_init_state(state_mlstm_C, state_mlstm_n, state_mlstm_m, j, 0),
                                          _ml_init_state(state_mlstm_C, state_mlstm_n, state_mlstm_m, j, 1))
            states.append((st_f[:N_CTX_SEQ], st_b[:N_CTX_SEQ]))
            x = _ml_out(hf, hb, o, ml_norm_g[j].reshape(1, D), ml_w_out[j].astype(BF16), x, modt[l])
        elif kind == 1:
            w_in2, w_uq2, w_ukv2 = _mla_weights(mla_w_in[j], mla_w_uq[j], mla_w_ukv[j])
            cos2, sin2 = _rope_tables()
            qn, qr, ckv, kpe = _mla_in(x, modt[l], n1, w_in2, mla_q_norm_g[j].reshape(1, Q_LORA),
                                       mla_kv_norm_g[j].reshape(1, KV_LORA), w_uq2, cos2, sin2)
            new_ckv, new_kpe = ckv[:N_CTX], kpe[:N_CTX]
            kn_tok, v_tok = _kv_expand(ckv, w_ukv2)
            kn_past, v_past = _kv_expand(cache_mla_ckv[:, j].reshape(N_LAT_SEQ * PAST_LEN, KV_LORA), w_ukv2)
            kr_past = cache_mla_kpe[:, j].reshape(N_LAT_SEQ * PAST_LEN, QK_ROPE)
            att = _attention(qn, qr, kn_tok, v_tok, kpe, kn_past, v_past, kr_past)
            x = _res_linear(att, mla_w_out[j].astype(BF16), x, modt[l])
        else:
            x = _fourier(x, modt[l], n1, _dft_tables(), fn_w_out[j].astype(BF16))
        x = _moe(l, x, modt[l], norm2_g[l].reshape(1, D), router_w[l], router_b[l],
                 exp_w1, exp_b1, exp_w2, exp_b2, final_g.reshape(1, D), l == DEPTH - 1)

    y_prompt = x[:N_CTX].reshape(N_CTX_SEQ, CTX_T, D)
    y_sample = x[N_CTX:].reshape(N_LAT_SEQ, LAT_T, D)
    new_mla_ckv = new_ckv.reshape(N_CTX_SEQ, 1, CTX_T, KV_LORA)
    new_mla_kpe = new_kpe.reshape(N_CTX_SEQ, 1, CTX_T, QK_ROPE)
    st = jnp.stack([jnp.stack([sf, sb], axis=1) for sf, sb in states], axis=1)
    new_c = st[..., 0:ML_DV]
    new_n = st[..., ML_N_LANE]
    new_m = st[..., 0, ML_M_LANE]
    return (y_prompt, y_sample, new_mla_ckv, new_mla_kpe, new_c, new_n, new_m)
```

```python
import functools

import numpy as np
import jax
import jax.numpy as jnp
from jax import lax
from jax.experimental import pallas as pl
from jax.experimental.pallas import tpu as pltpu

F32 = jnp.float32
BF16 = jnp.bfloat16

D = 1024
DEPTH = 4
N_CTX_SEQ, CTX_T = 16, 256
N_LAT_SEQ, LAT_T = 2, 2048
N_CTX = N_CTX_SEQ * CTX_T
N_TOK = N_CTX + N_LAT_SEQ * LAT_T
TM = 256
N_TILES = N_TOK // TM
CTX_TILES = N_CTX // TM
LAT_TILES = LAT_T // TM
N_SEQ = N_CTX_SEQ + N_LAT_SEQ
PAST_LEN = 512
GRID_W = 64
RMS_EPS = 1e-6

ML_HEADS, ML_DK, ML_DV = 8, 64, 128
ML_AUG = 256
ML_N_LANE, ML_M_LANE = ML_DV, ML_DV + 1

MLA_HEADS, QK_NOPE, QK_ROPE, V_HEAD = 16, 128, 64, 128
Q_LORA, KV_LORA = 384, 256
ROPE_THETA = 10000.0
ATT_HG = 4
ATT_SCALE = (QK_NOPE + QK_ROPE) ** -0.5

FN_GROUPS, FN_GW = 4, 256

N_EXPERTS, TOP_K = 32, 4
SWIGLU_ALPHA, SWIGLU_LIMIT = 1.702, 7.0
MOE_BM = 256
N_ASSIGN = N_TOK * TOP_K
MOE_NB = N_ASSIGN // MOE_BM + N_EXPERTS
MOE_STEPS = MOE_NB + 1
MOE_SLOTS = 3
MOE_HC = 256
MOE_NHC = D // MOE_HC
MOE_RPC = MOE_BM // MOE_NHC
Y4_ROWS = N_ASSIGN + MOE_SLOTS * MOE_BM

VMEM_LIMIT = 56 * 1024 * 1024


def _cp(n_grid_axes, vmem=None, **kw):
    return pltpu.CompilerParams(dimension_semantics=("arbitrary",) * n_grid_axes, vmem_limit_bytes=vmem, **kw)


def _dot(a, b):
    return jnp.dot(a, b, preferred_element_type=F32)


def _dot_nt(a, b):
    return lax.dot_general(a, b, (((1,), (1,)), ((), ())), preferred_element_type=F32)


def _dot_tn(a, b):
    return lax.dot_general(a, b, (((0,), (0,)), ((), ())), preferred_element_type=F32)


def _sigmoid(x):
    return 1.0 / (1.0 + jnp.exp(-x))


def _log_sigmoid(x):
    return jnp.minimum(x, 0.0) - jnp.log(1.0 + jnp.exp(-jnp.abs(x)))


def _rms(x):
    return x * lax.rsqrt(jnp.mean(x * x, axis=-1, keepdims=True) + RMS_EPS)


def _norm_mod(x, g, shift, scale):
    return (_rms(x) * g) * (1.0 + scale) + shift


def _split2(x):
    hi = x.astype(BF16)
    lo = (x - hi.astype(F32)).astype(BF16)
    return hi, lo


def _split3(x):
    hi = x.astype(BF16)
    r = x - hi.astype(F32)
    mid = r.astype(BF16)
    lo = (r - mid.astype(F32)).astype(BF16)
    return hi, mid, lo


LANE_CHUNKS = D // 128


def _store_token_tiles(ref, row0, val):
    n = val.shape[0]
    for c in range(LANE_CHUNKS):
        ref[pl.ds(row0 * LANE_CHUNKS + c, n, stride=LANE_CHUNKS), :] = val[:, c * 128:(c + 1) * 128]


def _load_token_tile_chunk(ref, row0, n, c):
    return ref[pl.ds(row0 * LANE_CHUNKS + c, n, stride=LANE_CHUNKS), :]


def _tile_seq(r):
    return jnp.where(r < CTX_TILES, r, CTX_TILES + (r - CTX_TILES) // LAT_TILES)


def _lat_seq(r):
    return jnp.clip((r - CTX_TILES) // LAT_TILES, 0, N_LAT_SEQ - 1)


MOD_TN = 1536


def _mod_kernel(c_ref, w_ref, b_ref, o_ref):
    a = c_ref[...]
    s_hi, s_lo = _split2(a * _sigmoid(a))
    w_hi, w_lo = _split2(w_ref[0])
    o_ref[0] = _dot_split(s_hi, s_lo, w_hi, w_lo) + b_ref[0]


def _modulation(cond8, ada_w, ada_b):
    n_col = ada_w.shape[-1]
    return pl.pallas_call(
        _mod_kernel,
        out_shape=jax.ShapeDtypeStruct((DEPTH, 8, n_col), F32),
        grid=(DEPTH, n_col // MOD_TN),
        in_specs=[pl.BlockSpec((8, D), lambda l, j: (0, 0)),
                  pl.BlockSpec((1, D, MOD_TN), lambda l, j: (l, 0, j)),
                  pl.BlockSpec((1, 1, MOD_TN), lambda l, j: (l, 0, j))],
        out_specs=pl.BlockSpec((1, 8, MOD_TN), lambda l, j: (l, 0, j)),
        compiler_params=_cp(2, VMEM_LIMIT),
    )(cond8, ada_w, ada_b.reshape(DEPTH, 1, n_col))


def _ml_in_kernel(x_ref, mod_ref, g_ref, w_ref, wg_ref, wgt_ref, qkv_ref, o_ref, gg_ref, gt_ref):
    h = _norm_mod(x_ref[...], g_ref[...], mod_ref[0, 0:1, :], mod_ref[0, 1:2, :]).astype(BF16)
    hk = ML_HEADS * ML_DK
    lane = lax.broadcasted_iota(jnp.int32, (1, 2 * hk), 1)
    qscale = jnp.where(lane < hk, ML_DK ** -0.5, 1.0).astype(F32)
    qkv_ref[:, 0:2 * hk] = (_dot(h, w_ref[:, 0:2 * hk]) * qscale).astype(BF16)
    qkv_ref[:, 2 * hk:2 * hk + D] = _dot(h, w_ref[:, 2 * hk:2 * hk + D]).astype(BF16)
    o_ref[...] = _dot(h, w_ref[:, 2 * hk + D:2 * hk + 2 * D])
    gg_ref[...] = _dot(h, wg_ref[...])
    gt_ref[...] = _dot_nt(wgt_ref[...], h)


def _ml_in(x, modt, norm_g, w_main, w_g, w_gt):
    tile = lambda i: (i, 0)
    const = lambda i: (0, 0)
    return pl.pallas_call(
        _ml_in_kernel,
        out_shape=(jax.ShapeDtypeStruct((N_TOK, 2 * D), BF16),
                   jax.ShapeDtypeStruct((N_TOK, D), F32),
                   jax.ShapeDtypeStruct((N_TOK, 128), F32),
                   jax.ShapeDtypeStruct((32, N_TOK), F32)),
        grid=(N_TILES,),
        in_specs=[pl.BlockSpec((TM, D), tile),
                  pl.BlockSpec((1, 6, D), lambda i: (i, 0, 0)),
                  pl.BlockSpec((1, D), const),
                  pl.BlockSpec((D, 3 * D), const),
                  pl.BlockSpec((D, 128), const),
                  pl.BlockSpec((32, D), const)],
        out_specs=(pl.BlockSpec((TM, 2 * D), tile),
                   pl.BlockSpec((TM, D), tile),
                   pl.BlockSpec((TM, 128), tile),
                   pl.BlockSpec((32, TM), lambda i: (0, i))),
        compiler_params=_cp(1, VMEM_LIMIT),
    )(x, modt, norm_g, w_main, w_g, w_gt)


def _ml_direction(lower, q_ref, k_ref, v_ref, g_ref, gt_ref, gbrow_ref, gbcol_ref, h_ref, st_ref, c_s, m_s):
    L = TM
    d_off = 0 if lower else ML_HEADS
    row = lax.broadcasted_iota(jnp.int32, (L, L), 0)
    col = lax.broadcasted_iota(jnp.int32, (L, L), 1)
    mask = (col <= row) if lower else (col >= row)
    tri = jnp.where(mask, 1.0, 0.0).astype(BF16)

    lane = lax.broadcasted_iota(jnp.int32, (L, 128), 1)
    gc = g_ref[...] + gbrow_ref[...]
    gc = jnp.where((lane >= 2 * ML_HEADS) & (lane < 4 * ML_HEADS), _log_sigmoid(gc), gc)
    bc = sum(_dot(tri, p) for p in _split3(gc))
    sub = lax.broadcasted_iota(jnp.int32, (32, L), 0)
    gr = gt_ref[...] + gbcol_ref[:, 0:1]
    gr = jnp.where(sub >= 2 * ML_HEADS, _log_sigmoid(gr), gr)
    br = sum(_dot_nt(p, tri) for p in _split3(gr))

    q = q_ref[...]
    k = k_ref[...]
    v = v_ref[...]
    lane_a = lax.broadcasted_iota(jnp.int32, (L, ML_AUG - ML_DV), 1)
    ones_blk = jnp.where(lane_a == 0, 1.0, 0.0).astype(BF16)
    lane_s = lax.broadcasted_iota(jnp.int32, (ML_DK, ML_AUG), 1)

    for hd in range(ML_HEADS):
        ci = d_off + hd
        cf = 2 * ML_HEADS + d_off + hd
        i_col, b_col = gc[:, ci:ci + 1], bc[:, cf:cf + 1]
        i_row, b_row = gr[ci:ci + 1, :], br[cf:cf + 1, :]
        total = b_col[L - 1:L, :] if lower else b_col[0:1, :]
        m = m_s[hd][0:1, 0:1]
        c_aug = c_s[hd]
        dmat = jnp.where(mask, (b_col - b_row) + i_row, -jnp.inf)
        mi = jnp.max(dmat, axis=1, keepdims=True)
        a_col = b_col + m
        m_row = jnp.maximum(a_col, mi)
        qh = q[:, hd * ML_DK:(hd + 1) * ML_DK]
        kh = k[:, hd * ML_DK:(hd + 1) * ML_DK]
        v_aug = jnp.concatenate([v[:, hd * ML_DV:(hd + 1) * ML_DV], ones_blk], axis=1)
        p = (_dot_nt(qh, kh) * jnp.exp(dmat - mi)).astype(BF16)
        num = jnp.exp(mi - m_row) * _dot(p, v_aug) + jnp.exp(a_col - m_row) * _dot(qh, c_aug.astype(BF16))
        den = jnp.maximum(jnp.abs(num[:, ML_N_LANE:ML_N_LANE + 1]), jnp.exp(-m_row))
        h_ref[:, hd * ML_DV:(hd + 1) * ML_DV] = num[:, 0:ML_DV] / den

        g_col = (total - b_col) + i_col
        m_new = jnp.maximum(total + m, jnp.max(g_col, axis=0, keepdims=True))
        kw = (kh.astype(F32) * jnp.exp(g_col - m_new)).astype(BF16)
        c_new = jnp.exp(total + m - m_new) * c_aug + _dot_tn(kw, v_aug)
        c_s[hd] = c_new
        m_s[hd] = jnp.broadcast_to(m_new, (8, 128))
        st_ref[0, hd] = jnp.where(lane_s == ML_M_LANE, m_new, c_new)


def _ml_core_kernel(qf, kf, vf, gf, gtf, qb, kb, vb, gb, gtb, gbrow, gbcol, initf, initb,
                    hf_ref, hb_ref, stf_ref, stb_ref, cf_s, cb_s, mf_s, mb_s):
    i = pl.program_id(0)
    start_f = (i <= CTX_TILES) | (i == CTX_TILES + LAT_TILES)
    start_b = (i == 0) | (i == LAT_TILES) | (i >= 2 * LAT_TILES)

    def load_state(init_ref, c_s, m_s):
        c_s[...] = init_ref[0]
        for hd in range(ML_HEADS):
            m_s[hd] = jnp.broadcast_to(init_ref[0, hd][0:1, ML_M_LANE:ML_M_LANE + 1], (8, 128))

    @pl.when(start_f)
    def _():
        load_state(initf, cf_s, mf_s)

    @pl.when(start_b)
    def _():
        load_state(initb, cb_s, mb_s)

    _ml_direction(True, qf, kf, vf, gf, gtf, gbrow, gbcol, hf_ref, stf_ref, cf_s, mf_s)
    _ml_direction(False, qb, kb, vb, gb, gtb, gbrow, gbcol, hb_ref, stb_ref, cb_s, mb_s)


def _ml_core(qkv, gg, gt, gb_row, gb_col, init_f, init_b):
    last = N_TILES - 1
    fwd = lambda i: i
    bwd = lambda i: last - i
    hk = ML_HEADS * ML_DK

    def specs(t):
        return [pl.BlockSpec((TM, hk), lambda i: (t(i), 0)),
                pl.BlockSpec((TM, hk), lambda i: (t(i), 1)),
                pl.BlockSpec((TM, D), lambda i: (t(i), 1)),
                pl.BlockSpec((TM, 128), lambda i: (t(i), 0)),
                pl.BlockSpec((32, TM), lambda i: (0, t(i)))]

    def st_spec(t):
        return pl.BlockSpec((1, ML_HEADS, ML_DK, ML_AUG), lambda i: (_tile_seq(t(i)), 0, 0, 0))

    st_shape = jax.ShapeDtypeStruct((N_SEQ, ML_HEADS, ML_DK, ML_AUG), F32)
    return pl.pallas_call(
        _ml_core_kernel,
        out_shape=(jax.ShapeDtypeStruct((N_TOK, D), F32), jax.ShapeDtypeStruct((N_TOK, D), F32),
                   st_shape, st_shape),
        grid=(N_TILES,),
        in_specs=specs(fwd) + specs(bwd) + [
            pl.BlockSpec((1, 128), lambda i: (0, 0)),
            pl.BlockSpec((32, 128), lambda i: (0, 0)),
            st_spec(fwd), st_spec(bwd)],
        out_specs=(pl.BlockSpec((TM, D), lambda i: (i, 0)),
                   pl.BlockSpec((TM, D), lambda i: (last - i, 0)),
                   st_spec(fwd), st_spec(bwd)),
        scratch_shapes=[pltpu.VMEM((ML_HEADS, ML_DK, ML_AUG), F32),
                        pltpu.VMEM((ML_HEADS, ML_DK, ML_AUG), F32),
                        pltpu.VMEM((ML_HEADS, 8, 128), F32),
                        pltpu.VMEM((ML_HEADS, 8, 128), F32)],
        compiler_params=_cp(1, VMEM_LIMIT),
    )(qkv, qkv, qkv, gg, gt, qkv, qkv, qkv, gg, gt, gb_row, gb_col, init_f, init_b)


def _ml_out_kernel(hf_ref, hb_ref, o_ref, ng_ref, w_ref, x_ref, mod_ref, out_ref, z_s):
    hh = hf_ref[...] + hb_ref[...]
    o = o_ref[...]
    for hd in range(ML_HEADS):
        sl = slice(hd * ML_DV, (hd + 1) * ML_DV)
        z_s[:, sl] = (_sigmoid(o[:, sl]) * (_rms(hh[:, sl]) * ng_ref[:, sl])).astype(BF16)
    out_ref[...] = x_ref[...] + mod_ref[0, 2:3, :] * _dot(z_s[...], w_ref[...])


def _ml_out(hf, hb, o, norm_g, w_out, x, modt):
    tile = lambda i: (i, 0)
    const = lambda i: (0, 0)
    return pl.pallas_call(
        _ml_out_kernel,
        out_shape=jax.ShapeDtypeStruct((N_TOK, D), F32),
        grid=(N_TILES,),
        in_specs=[pl.BlockSpec((TM, D), tile), pl.BlockSpec((TM, D), tile), pl.BlockSpec((TM, D), tile),
                  pl.BlockSpec((1, D), const), pl.BlockSpec((D, D), const),
                  pl.BlockSpec((TM, D), tile), pl.BlockSpec((1, 6, D), lambda i: (i, 0, 0))],
        out_specs=pl.BlockSpec((TM, D), tile),
        scratch_shapes=[pltpu.VMEM((TM, D), BF16)],
        compiler_params=_cp(1, VMEM_LIMIT),
    )(hf, hb, o, norm_g, w_out, x, modt)


MLA_IN_COLS = 896
NOPE_ALL = MLA_HEADS * QK_NOPE
ROPE_ALL = MLA_HEADS * QK_ROPE


def _mla_in_kernel(x_ref, mod_ref, g_ref, w_ref, qg_ref, kvg_ref, wuq_ref, cos_ref, sin_ref,
                   qn_ref, qr_ref, ckv_ref, kpe_ref):
    h = _norm_mod(x_ref[...], g_ref[...], mod_ref[0, 0:1, :], mod_ref[0, 1:2, :]).astype(BF16)
    c = _dot(h, w_ref[...])
    cos, sin = cos_ref[...], sin_ref[...]
    ckv_ref[...] = _rms(c[:, Q_LORA:Q_LORA + KV_LORA]) * kvg_ref[...]
    kpe_ref[...] = c[:, 640:704] * cos[:, 0:QK_ROPE] + c[:, 768:832] * sin[:, 0:QK_ROPE]
    cq = (_rms(c[:, 0:Q_LORA]) * qg_ref[...]).astype(BF16)
    qn_ref[...] = _dot(cq, wuq_ref[:, 0:NOPE_ALL]).astype(BF16)
    rope = _dot(cq, wuq_ref[:, NOPE_ALL:NOPE_ALL + ROPE_ALL])
    swapped = _dot(cq, wuq_ref[:, NOPE_ALL + ROPE_ALL:NOPE_ALL + 2 * ROPE_ALL])
    for s in range(ROPE_ALL // 128):
        sl = slice(s * 128, (s + 1) * 128)
        qr_ref[:, sl] = (rope[:, sl] * cos + swapped[:, sl] * sin).astype(BF16)


def _mla_in(x, modt, norm_g, w_in2, qg, kvg, w_uq2, cos2, sin2):
    tile = lambda i: (i, 0)
    const = lambda i: (0, 0)
    return pl.pallas_call(
        _mla_in_kernel,
        out_shape=(jax.ShapeDtypeStruct((N_TOK, NOPE_ALL), BF16),
                   jax.ShapeDtypeStruct((N_TOK, ROPE_ALL), BF16),
                   jax.ShapeDtypeStruct((N_TOK, KV_LORA), F32),
                   jax.ShapeDtypeStruct((N_TOK, QK_ROPE), F32)),
        grid=(N_TILES,),
        in_specs=[pl.BlockSpec((TM, D), tile), pl.BlockSpec((1, 6, D), lambda i: (i, 0, 0)),
                  pl.BlockSpec((1, D), const), pl.BlockSpec((D, MLA_IN_COLS), const),
                  pl.BlockSpec((1, Q_LORA), const), pl.BlockSpec((1, KV_LORA), const),
                  pl.BlockSpec((Q_LORA, NOPE_ALL + 2 * ROPE_ALL), const),
                  pl.BlockSpec((TM, 128), tile), pl.BlockSpec((TM, 128), tile)],
        out_specs=(pl.BlockSpec((TM, NOPE_ALL), tile), pl.BlockSpec((TM, ROPE_ALL), tile),
                   pl.BlockSpec((TM, KV_LORA), tile), pl.BlockSpec((TM, QK_ROPE), tile)),
        compiler_params=_cp(1, VMEM_LIMIT),
    )(x, modt, norm_g, w_in2, qg, kvg, w_uq2, cos2, sin2)


def _kv_expand_kernel(c_ref, w_ref, kn_ref, v_ref):
    c = c_ref[...].astype(BF16)
    kn_ref[...] = _dot(c, w_ref[:, 0:NOPE_ALL]).astype(BF16)
    v_ref[...] = _dot(c, w_ref[:, NOPE_ALL:2 * NOPE_ALL]).astype(BF16)


def _kv_expand(ckv, w_ukv2):
    rows = ckv.shape[0]
    tile = lambda i: (i, 0)
    shp = jax.ShapeDtypeStruct((rows, NOPE_ALL), BF16)
    return pl.pallas_call(
        _kv_expand_kernel,
        out_shape=(shp, shp),
        grid=(rows // TM,),
        in_specs=[pl.BlockSpec((TM, KV_LORA), tile), pl.BlockSpec((KV_LORA, 2 * NOPE_ALL), lambda i: (0, 0))],
        out_specs=(pl.BlockSpec((TM, NOPE_ALL), tile), pl.BlockSpec((TM, NOPE_ALL), tile)),
        compiler_params=_cp(1, VMEM_LIMIT),
    )(ckv, w_ukv2)


def _attend(qn_ref, qr_ref, segs, out_ref):
    kr = [s[2][...].astype(BF16) for s in segs]
    for hd in range(ATT_HG):
        sl = slice(hd * QK_NOPE, (hd + 1) * QK_NOPE)
        qn = qn_ref[:, sl]
        qr = qr_ref[:, hd * QK_ROPE:(hd + 1) * QK_ROPE]
        scores = [(_dot_nt(qn, s[0][:, sl]) + _dot_nt(qr, kr_s)) * ATT_SCALE for s, kr_s in zip(segs, kr)]
        m = functools.reduce(jnp.maximum, [jnp.max(s, axis=1, keepdims=True) for s in scores])
        ps = [jnp.exp(s - m) for s in scores]
        denom = sum(jnp.sum(p, axis=1, keepdims=True) for p in ps)
        acc = sum(_dot(p.astype(BF16), s[1][:, sl]) for p, s in zip(ps, segs))
        out_ref[:, sl] = (acc / denom).astype(BF16)


def _attn_kernel(qn_ref, qr_ref, knc_ref, vc_ref, krc_ref, knl_ref, vl_ref, krl_ref, knp_ref, vp_ref, krp_ref,
                 out_ref):
    r = pl.program_id(1)

    @pl.when(r < CTX_TILES)
    def _():
        _attend(qn_ref, qr_ref, [(knc_ref, vc_ref, krc_ref)], out_ref)

    @pl.when(r >= CTX_TILES)
    def _():
        _attend(qn_ref, qr_ref, [(knp_ref, vp_ref, krp_ref), (knl_ref, vl_ref, krl_ref)], out_ref)


def _attention(qn, qr, kn_tok, v_tok, kr_tok, kn_past, v_past, kr_past):
    hw = ATT_HG * QK_NOPE
    ctx_t = lambda g, r: (jnp.minimum(r, CTX_TILES - 1), g)
    lat_t = lambda g, r: (N_CTX // LAT_T + _lat_seq(r), g)
    past_t = lambda g, r: (_lat_seq(r), g)
    col0 = lambda f: (lambda g, r: (f(g, r)[0], 0))
    return pl.pallas_call(
        _attn_kernel,
        out_shape=jax.ShapeDtypeStruct((N_TOK, NOPE_ALL), BF16),
        grid=(MLA_HEADS // ATT_HG, N_TILES),
        in_specs=[pl.BlockSpec((TM, hw), lambda g, r: (r, g)),
                  pl.BlockSpec((TM, ATT_HG * QK_ROPE), lambda g, r: (r, g)),
                  pl.BlockSpec((CTX_T, hw), ctx_t), pl.BlockSpec((CTX_T, hw), ctx_t),
                  pl.BlockSpec((CTX_T, QK_ROPE), col0(ctx_t)),
                  pl.BlockSpec((LAT_T, hw), lat_t), pl.BlockSpec((LAT_T, hw), lat_t),
                  pl.BlockSpec((LAT_T, QK_ROPE), col0(lat_t)),
                  pl.BlockSpec((PAST_LEN, hw), past_t), pl.BlockSpec((PAST_LEN, hw), past_t),
                  pl.BlockSpec((PAST_LEN, QK_ROPE), col0(past_t))],
        out_specs=pl.BlockSpec((TM, hw), lambda g, r: (r, g)),
        compiler_params=_cp(2, VMEM_LIMIT),
    )(qn, qr, kn_tok, v_tok, kr_tok, kn_tok, v_tok, kr_tok, kn_past, v_past, kr_past)


def _res_linear_kernel(a_ref, w_ref, x_ref, mod_ref, out_ref):
    out_ref[...] = x_ref[...] + mod_ref[0, 2:3, :] * _dot(a_ref[...], w_ref[...])


def _res_linear(a, w, x, modt):
    k = a.shape[1]
    tile = lambda i: (i, 0)
    return pl.pallas_call(
        _res_linear_kernel,
        out_shape=jax.ShapeDtypeStruct((N_TOK, D), F32),
        grid=(N_TILES,),
        in_specs=[pl.BlockSpec((TM, k), tile), pl.BlockSpec((k, D), lambda i: (0, 0)),
                  pl.BlockSpec((TM, D), tile), pl.BlockSpec((1, 6, D), lambda i: (i, 0, 0))],
        out_specs=pl.BlockSpec((TM, D), tile),
        compiler_params=_cp(1, VMEM_LIMIT),
    )(a, w, x, modt)


def _dot_split(a_hi, a_lo, b_hi, b_lo):
    return _dot(a_hi, b_hi) + (_dot(a_hi, b_lo) + _dot(a_lo, b_hi))


def _fn_channel_dft(x, g, shift, scale, wc_hi, wc_lo):
    h_hi, h_lo = _split2(_norm_mod(x, g, shift, scale))
    a_parts, b_parts = [], []
    for grp in range(FN_GROUPS):
        sl = slice(grp * FN_GW, (grp + 1) * FN_GW)
        ab = _dot_split(h_hi[:, sl], h_lo[:, sl], wc_hi, wc_lo)
        a_parts.append(ab[:, 0:FN_GW])
        b_parts.append(ab[:, FN_GW:2 * FN_GW])
    return jnp.concatenate(a_parts, axis=1), jnp.concatenate(b_parts, axis=1)


FN_STEPS = CTX_TILES + N_LAT_SEQ * 2 * LAT_TILES


def _fn_step(t):
    u = jnp.maximum(t - CTX_TILES, 0)
    seq, ph, tile = u // (2 * LAT_TILES), (u % (2 * LAT_TILES)) // LAT_TILES, u % LAT_TILES
    is_ctx = t < CTX_TILES
    return is_ctx, ph, tile, jnp.where(is_ctx, t, CTX_TILES + seq * LAT_TILES + tile)


def _fn_kernel(x_ref, mod_ref, g_ref, wc_ref, tc_ref, tl_ref, w_ref, out_ref, ab_hi_s, ab_lo_s):
    is_ctx, ph, tile, _ = _fn_step(pl.program_id(0))
    g = g_ref[...]
    shift, scale, gate = mod_ref[0, 0:1, :], mod_ref[0, 1:2, :], mod_ref[0, 2:3, :]
    wc_hi, wc_lo = _split2(wc_ref[...])

    def finish(f, t_len):
        f = f * ((t_len * FN_GW) ** -0.5)
        out_ref[...] = x_ref[...] + gate * _dot(f.astype(BF16), w_ref[...])

    @pl.when(is_ctx)
    def _():
        a, b = _fn_channel_dft(x_ref[...], g, shift, scale, wc_hi, wc_lo)
        ab_hi, ab_lo = _split2(jnp.concatenate([a, b], axis=0))
        finish(_dot_split(*_split2(tc_ref[...]), ab_hi, ab_lo), CTX_T)

    @pl.when(jnp.logical_not(is_ctx) & (ph == 0))
    def _():
        a, b = _fn_channel_dft(x_ref[...], g, shift, scale, wc_hi, wc_lo)
        a_hi, a_lo = _split2(a)
        b_hi, b_lo = _split2(b)
        rows = pl.ds(pl.multiple_of(tile * TM, TM), TM)
        rows_b = pl.ds(pl.multiple_of(LAT_T + tile * TM, TM), TM)
        ab_hi_s[rows, :] = a_hi
        ab_lo_s[rows, :] = a_lo
        ab_hi_s[rows_b, :] = b_hi
        ab_lo_s[rows_b, :] = b_lo

    @pl.when(jnp.logical_not(is_ctx) & (ph == 1))
    def _():
        finish(_dot_split(*_split2(tl_ref[...]), ab_hi_s[...], ab_lo_s[...]), LAT_T)


def _fourier(x, modt, norm_g, tabs, w_out):
    wc, tc, tl = tabs
    const = lambda t: (0, 0)
    x_tile = lambda t: (_fn_step(t)[3], 0)

    def out_tile(t):
        is_ctx, ph, tile, gt = _fn_step(t)
        return (jnp.where(is_ctx | (ph == 1), gt, gt - tile), 0)

    def tab_row(t):
        is_ctx, ph, tile, _ = _fn_step(t)
        return (jnp.where(is_ctx | (ph == 0), 0, tile), 0)

    return pl.pallas_call(
        _fn_kernel,
        out_shape=jax.ShapeDtypeStruct((N_TOK, D), F32),
        grid=(FN_STEPS,),
        in_specs=[pl.BlockSpec((TM, D), x_tile),
                  pl.BlockSpec((1, 6, D), lambda t: (_fn_step(t)[3], 0, 0)),
                  pl.BlockSpec((1, D), const),
                  pl.BlockSpec((FN_GW, 2 * FN_GW), const),
                  pl.BlockSpec((CTX_T, 2 * CTX_T), const),
                  pl.BlockSpec((TM, 2 * LAT_T), tab_row),
                  pl.BlockSpec((D, D), const)],
        out_specs=pl.BlockSpec((TM, D), out_tile),
        scratch_shapes=[pltpu.VMEM((2 * LAT_T, D), BF16), pltpu.VMEM((2 * LAT_T, D), BF16)],
        compiler_params=_cp(1, VMEM_LIMIT),
    )(x, modt, norm_g, wc, tc, tl, w_out)


@functools.lru_cache(maxsize=None)
def _dft_tables():
    def cos_sin(n):
        k = np.arange(n, dtype=np.int64)
        ang = ((k[:, None] * k[None, :]) % n).astype(np.float64) * (2.0 * np.pi / n)
        return np.cos(ang), np.sin(ang)

    cc, sc = cos_sin(FN_GW)
    out = [np.concatenate([cc, sc], axis=1).astype(np.float32)]
    for t_len in (CTX_T, LAT_T):
        ct, st = cos_sin(t_len)
        out.append(np.concatenate([ct, -st], axis=1).astype(np.float32))
    return tuple(out)


def _router_kernel(x_ref, mod_ref, g_ref, rw_hi_ref, rw_lo_ref, rb_ref, h_ref, idx_ref, gate_ref):
    h = _norm_mod(x_ref[...], g_ref[...], mod_ref[0, 3:4, :], mod_ref[0, 4:5, :])
    _store_token_tiles(h_ref, 0, h)
    h_hi, h_lo = _split2(h)
    logits = _dot_split(h_hi, h_lo, rw_hi_ref[...], rw_lo_ref[...]) + rb_ref[...]
    lane = lax.broadcasted_iota(jnp.int32, logits.shape, 1)
    lane_f = lane.astype(F32)
    cur = jnp.where(lane < N_EXPERTS, logits, -jnp.inf)
    vals, idxs = [], []
    for _ in range(TOP_K):
        m = jnp.max(cur, axis=1, keepdims=True)
        ik = jnp.min(jnp.where(cur == m, lane_f, 128.0), axis=1, keepdims=True).astype(jnp.int32)
        vals.append(m)
        idxs.append(ik)
        cur = jnp.where(lane == ik, -jnp.inf, cur)
    es = [jnp.exp(v - vals[0]) for v in vals]
    denom = functools.reduce(lambda a, b: a + b, es)
    idx_out = jnp.zeros(logits.shape, jnp.int32)
    gate_out = jnp.zeros(logits.shape, F32)
    for kk in range(TOP_K):
        idx_out = jnp.where(lane == kk, idxs[kk], idx_out)
        gate_out = jnp.where(lane == kk, es[kk] / denom, gate_out)
    idx_ref[...] = idx_out
    gate_ref[...] = gate_out


def _router(x, modt, norm_g, rw_hi, rw_lo, rb):
    tile = lambda i: (i, 0)
    const = lambda i: (0, 0)
    return pl.pallas_call(
        _router_kernel,
        out_shape=(jax.ShapeDtypeStruct((N_TOK * LANE_CHUNKS, 128), F32),
                   jax.ShapeDtypeStruct((N_TOK, 128), jnp.int32),
                   jax.ShapeDtypeStruct((N_TOK, 128), F32)),
        grid=(N_TILES,),
        in_specs=[pl.BlockSpec((TM, D), tile), pl.BlockSpec((1, 6, D), lambda i: (i, 0, 0)),
                  pl.BlockSpec((1, D), const), pl.BlockSpec((D, 128), const), pl.BlockSpec((D, 128), const),
                  pl.BlockSpec((1, 128), const)],
        out_specs=(pl.BlockSpec((TM * LANE_CHUNKS, 128), tile), pl.BlockSpec((TM, 128), tile),
                   pl.BlockSpec((TM, 128), tile)),
        compiler_params=_cp(1, VMEM_LIMIT),
    )(x, modt, norm_g, rw_hi, rw_lo, rb)


def _route_plan(idx):
    ids = idx[:, 0:TOP_K].reshape(-1)
    order = jnp.argsort(ids, stable=True).astype(jnp.int32)
    counts = jnp.sum((ids[:, None] == jnp.arange(N_EXPERTS, dtype=jnp.int32)[None, :]).astype(jnp.int32), axis=0)
    starts = jnp.cumsum(counts) - counts
    nblk = (counts + MOE_BM - 1) // MOE_BM
    pb_end = jnp.cumsum(nblk)
    pb_start = pb_end - nblk
    total = pb_end[-1]
    b = jnp.arange(MOE_STEPS, dtype=jnp.int32)
    be = jnp.minimum(jnp.sum((pb_end[None, :] <= b[:, None]).astype(jnp.int32), axis=1), N_EXPERTS - 1)
    be = jnp.where(b < total, be, be[jnp.maximum(total - 1, 0)])
    local = b - pb_start[be]
    base = starts[be] + local * MOE_BM
    nvalid = jnp.clip(counts[be] - local * MOE_BM, 0, MOE_BM)
    first = ((local == 0) & (b < total)).astype(jnp.int32)
    return (be.astype(jnp.int32), first, base.astype(jnp.int32), nvalid.astype(jnp.int32),
            total.reshape(1).astype(jnp.int32), order)


def _expert_kernel(be_ref, first_ref, base_ref, nv_ref, total_ref, order_ref,
                   x_hbm, w1_ref, b1_ref, w2_ref, b2_ref, y4_hbm,
                   xs_buf, y_buf, y_acc, xs_bf, w1b, w2b, gsem, ssem):
    b = pl.program_id(0)
    total = total_ref[0]

    def tile_rows(r):
        return pl.ds(pl.multiple_of(r * LANE_CHUNKS, LANE_CHUNKS), LANE_CHUNKS)

    def slot_rows(slot):
        return pl.ds(pl.multiple_of(slot * (MOE_BM * LANE_CHUNKS), MOE_BM * LANE_CHUNKS), MOE_BM * LANE_CHUNKS)

    def gather_start(base, nv, slot, row0, i):
        a = order_ref[base + jnp.minimum(row0 + i, nv - 1)]
        pltpu.make_async_copy(x_hbm.at[tile_rows(a >> 2), :], xs_buf.at[tile_rows(slot * MOE_BM + row0 + i), :],
                              gsem.at[slot]).start()

    def scatter_start(base, nv, slot, row0, i):
        row = row0 + i
        a = order_ref[base + jnp.maximum(jnp.minimum(row, nv - 1), 0)]
        dst = jnp.where(row < nv, (a & (TOP_K - 1)) * N_TOK + (a >> 2), N_ASSIGN + slot * MOE_BM + row)
        pltpu.make_async_copy(y_buf.at[tile_rows(slot * MOE_BM + row), :], y4_hbm.at[tile_rows(dst), :],
                              ssem.at[slot]).start()

    def wait_gather(slot):
        pltpu.make_async_copy(x_hbm.at[slot_rows(0), :], xs_buf.at[slot_rows(slot), :], gsem.at[slot]).wait()

    def wait_scatter(slot):
        pltpu.make_async_copy(y_buf.at[slot_rows(slot), :], y4_hbm.at[slot_rows(0), :], ssem.at[slot]).wait()

    def row_loop(fn):
        def body(g, c):
            for i in range(8):
                fn(g * 8, i)
            return c
        lax.fori_loop(0, MOE_BM // 8, body, 0)

    @pl.when(b == 0)
    def _():
        y_buf[...] = jnp.zeros(y_buf.shape, F32)
        for s in range(2):
            pltpu.make_async_copy(y_buf.at[slot_rows(s), :], y4_hbm.at[slot_rows(N_ASSIGN // MOE_BM + s), :],
                                  ssem.at[s]).start()
        row_loop(functools.partial(gather_start, base_ref[0], nv_ref[0], 0))
        blk1 = jnp.minimum(1, total - 1)
        row_loop(functools.partial(gather_start, base_ref[blk1], nv_ref[blk1], 1))

    @pl.when(b < total)
    def _():
        slot = b % MOE_SLOTS

        @pl.when(first_ref[b] == 1)
        def _():
            for j in range(2 * MOE_NHC):
                w1b[j] = w1_ref[0, 0, :, j * MOE_HC:(j + 1) * MOE_HC].astype(BF16)
            for j in range(MOE_NHC):
                w2b[j] = w2_ref[0, 0, j * MOE_HC:(j + 1) * MOE_HC, :].astype(BF16)

        wait_gather(slot)
        wait_scatter(slot)
        for c in range(LANE_CHUNKS):
            xs_bf[:, c * 128:(c + 1) * 128] = _load_token_tile_chunk(xs_buf, slot * MOE_BM, MOE_BM, c).astype(BF16)
        y_acc[...] = jnp.broadcast_to(b2_ref[0, 0], (MOE_BM, D))

        nblk = jnp.minimum(b + 2, total - 1)
        nbase, nnv = base_ref[nblk], nv_ref[nblk]
        pblk = jnp.maximum(b - 1, 0)
        pbase, pnv = base_ref[pblk], jnp.where(b >= 1, nv_ref[pblk], 0)
        oslot = (b + 2) % MOE_SLOTS

        def hidden_chunk(j, c):
            xb = xs_bf[...]
            hg = _dot(xb, w1b[j]) + b1_ref[0, 0, pl.ds(j, 1), :]
            hu = _dot(xb, w1b[MOE_NHC + j]) + b1_ref[0, 0, pl.ds(MOE_NHC + j, 1), :]
            gate = jnp.minimum(hg, SWIGLU_LIMIT)
            up = jnp.clip(hu, -SWIGLU_LIMIT, SWIGLU_LIMIT)
            act = ((up + 1.0) * (gate * _sigmoid(SWIGLU_ALPHA * gate))).astype(BF16)
            y_acc[...] += _dot(act, w2b[j])
            for i in range(MOE_RPC):
                gather_start(nbase, nnv, oslot, j * MOE_RPC, i)
                scatter_start(pbase, pnv, oslot, j * MOE_RPC, i)
            return c
        lax.fori_loop(0, MOE_NHC, hidden_chunk, 0)
        _store_token_tiles(y_buf, slot * MOE_BM, y_acc[...])

    @pl.when(b == total)
    def _():
        last = total - 1
        row_loop(functools.partial(scatter_start, base_ref[last], nv_ref[last], last % MOE_SLOTS))

    @pl.when(b == MOE_STEPS - 1)
    def _():
        for s in range(MOE_SLOTS):
            wait_scatter(s)
        wait_gather(total % MOE_SLOTS)
        wait_gather((total + 1) % MOE_SLOTS)


def _experts(layer, plan, h2, w1, b1, w2, b2):
    be, first, base, nvalid, total, order = plan
    wmap = lambda b, be, *_: (layer, be[b], 0, 0)
    grid_spec = pltpu.PrefetchScalarGridSpec(
        num_scalar_prefetch=6,
        grid=(MOE_STEPS,),
        in_specs=[pl.BlockSpec(memory_space=pl.ANY),
                  pl.BlockSpec((1, 1, D, 2 * D), wmap), pl.BlockSpec((1, 1, 2 * MOE_NHC, MOE_HC), wmap),
                  pl.BlockSpec((1, 1, D, D), wmap), pl.BlockSpec((1, 1, 1, D), wmap)],
        out_specs=pl.BlockSpec(memory_space=pl.ANY),
        scratch_shapes=[pltpu.VMEM((MOE_SLOTS * MOE_BM * LANE_CHUNKS, 128), F32),
                        pltpu.VMEM((MOE_SLOTS * MOE_BM * LANE_CHUNKS, 128), F32),
                        pltpu.VMEM((MOE_BM, D), F32), pltpu.VMEM((MOE_BM, D), BF16),
                        pltpu.VMEM((2 * MOE_NHC, D, MOE_HC), BF16), pltpu.VMEM((MOE_NHC, MOE_HC, D), BF16),
                        pltpu.SemaphoreType.DMA((MOE_SLOTS,)), pltpu.SemaphoreType.DMA((MOE_SLOTS,))],
    )
    return pl.pallas_call(
        _expert_kernel,
        out_shape=jax.ShapeDtypeStruct((Y4_ROWS * LANE_CHUNKS, 128), F32),
        grid_spec=grid_spec,
        compiler_params=_cp(1, VMEM_LIMIT, disable_bounds_checks=True),
    )(be, first, base, nvalid, total, order, h2, w1, b1.reshape(DEPTH, N_EXPERTS, 2 * MOE_NHC, MOE_HC), w2,
      b2.reshape(DEPTH, N_EXPERTS, 1, D))


def _combine_kernel(final, x_ref, y0, y1, y2, y3, gate_ref, mod_ref, fg_ref, out_ref):
    gates = gate_ref[...]
    parts = []
    for c in range(LANE_CHUNKS):
        moe = gates[:, 0:1] * _load_token_tile_chunk(y0, 0, TM, c)
        for kk, y in ((1, y1), (2, y2), (3, y3)):
            moe = moe + gates[:, kk:kk + 1] * _load_token_tile_chunk(y, 0, TM, c)
        parts.append(moe)
    x = x_ref[...] + mod_ref[0, 5:6, :] * jnp.concatenate(parts, axis=1)
    out_ref[...] = _rms(x) * fg_ref[...] if final else x


def _combine(x, y4, gate, modt, final_g, final):
    tile = lambda i: (i, 0)
    plane = lambda kk: pl.BlockSpec((TM * LANE_CHUNKS, 128), lambda i: (kk * N_TILES + i, 0))
    return pl.pallas_call(
        functools.partial(_combine_kernel, final),
        out_shape=jax.ShapeDtypeStruct((N_TOK, D), F32),
        grid=(N_TILES,),
        in_specs=[pl.BlockSpec((TM, D), tile), plane(0), plane(1), plane(2), plane(3),
                  pl.BlockSpec((TM, 128), tile), pl.BlockSpec((1, 6, D), lambda i: (i, 0, 0)),
                  pl.BlockSpec((1, D), lambda i: (0, 0))],
        out_specs=pl.BlockSpec((TM, D), tile),
        compiler_params=_cp(1, VMEM_LIMIT),
    )(x, y4, y4, y4, y4, gate, modt, final_g)


def _moe(layer, x, modt, norm2_g, router_w, router_b, w1, b1, w2, b2, final_g, final):
    rw = jnp.pad(router_w, ((0, 0), (0, 128 - N_EXPERTS)))
    rw_hi, rw_lo = _split2(rw)
    rb = jnp.pad(router_b, (0, 128 - N_EXPERTS)).reshape(1, 128)
    h2, idx, gate = _router(x, modt, norm2_g, rw_hi, rw_lo, rb)
    y4 = _experts(layer, _route_plan(idx), h2, w1, b1, w2, b2)
    return _combine(x, y4, gate, modt, final_g, final)


def _rope_tables():
    def tables(n_tokens):
        rows = n_tokens // GRID_W
        row = jnp.repeat(jnp.arange(rows, dtype=F32), GRID_W)
        col = jnp.tile(jnp.arange(GRID_W, dtype=F32), rows)
        n_freq = QK_ROPE // 4
        inv = ROPE_THETA ** (-jnp.arange(n_freq, dtype=F32) / n_freq)
        ang = jnp.stack([row[:, None] * inv, col[:, None] * inv], axis=1)
        return jnp.cos(ang), jnp.sin(ang)

    cos, sin = tables(LAT_T)
    cos64 = jnp.concatenate([cos[:, 0], cos[:, 0], cos[:, 1], cos[:, 1]], axis=1)
    sin64 = jnp.concatenate([-sin[:, 0], sin[:, 0], -sin[:, 1], sin[:, 1]], axis=1)
    cos_all = jnp.concatenate([jnp.ones((N_CTX, QK_ROPE), F32)] + [cos64] * N_LAT_SEQ, axis=0)
    sin_all = jnp.concatenate([jnp.zeros((N_CTX, QK_ROPE), F32)] + [sin64] * N_LAT_SEQ, axis=0)
    return jnp.tile(cos_all, (1, 2)), jnp.tile(sin_all, (1, 2))


_PAIR_SWAP = np.concatenate([np.arange(16, 32), np.arange(0, 16), np.arange(48, 64), np.arange(32, 48)])


def _mla_weights(w_in, w_uq, w_ukv):
    kpe = w_in[:, Q_LORA + KV_LORA:]
    pad = jnp.zeros((D, 64), F32)
    w_in2 = jnp.concatenate([w_in[:, :Q_LORA + KV_LORA], kpe, pad, kpe[:, _PAIR_SWAP], pad], axis=1)
    uq = w_uq.reshape(Q_LORA, MLA_HEADS, QK_NOPE + QK_ROPE)
    rope = uq[:, :, QK_NOPE:]
    w_uq2 = jnp.concatenate([uq[:, :, :QK_NOPE].reshape(Q_LORA, NOPE_ALL), rope.reshape(Q_LORA, ROPE_ALL),
                             rope[:, :, _PAIR_SWAP].reshape(Q_LORA, ROPE_ALL)], axis=1)
    ukv = w_ukv.reshape(KV_LORA, MLA_HEADS, QK_NOPE + V_HEAD)
    w_ukv2 = jnp.concatenate([ukv[:, :, :QK_NOPE].reshape(KV_LORA, NOPE_ALL),
                              ukv[:, :, QK_NOPE:].reshape(KV_LORA, NOPE_ALL)], axis=1)
    return w_in2.astype(BF16), w_uq2.astype(BF16), w_ukv2.astype(BF16)


def _ml_init_state(state_c, state_n, state_m, j, direction):
    c = state_c[:, j, direction].astype(F32)
    n = state_n[:, j, direction].astype(F32)[..., None]
    m = jnp.broadcast_to(state_m[:, j, direction].astype(F32)[..., None, None], n.shape)
    pad = jnp.zeros(c.shape[:-1] + (ML_AUG - ML_DV - 2,), F32)
    lat = jnp.concatenate([c, n, m, pad], axis=-1)
    return jnp.concatenate([jnp.zeros((N_CTX_SEQ,) + lat.shape[1:], F32), lat], axis=0)


def kernel(x_prompt, x_sample, cache_mla_ckv, cache_mla_kpe, state_mlstm_C, state_mlstm_n, state_mlstm_m, c, c_ctx, norm1_g, norm2_g, ada_w, ada_b, ml_w_in, ml_gate_b, ml_norm_g, ml_w_out, mla_w_in, mla_q_norm_g, mla_w_uq, mla_kv_norm_g, mla_w_ukv, mla_w_out, fn_w_out, router_w, router_b, exp_w1, exp_b1, exp_w2, exp_b2, final_g):
    x = jnp.concatenate([x_prompt.reshape(N_CTX, D), x_sample.reshape(N_LAT_SEQ * LAT_T, D)], axis=0)

    cond8 = jnp.concatenate([c_ctx[None, :], c, jnp.zeros((8 - 1 - N_LAT_SEQ, D), F32)], axis=0)
    mod = _modulation(cond8, ada_w, ada_b)
    tile_cond = np.concatenate([np.zeros(CTX_TILES, np.int32)] +
                               [np.full(LAT_TILES, 1 + s, np.int32) for s in range(N_LAT_SEQ)])
    modt = mod[:, tile_cond].reshape(DEPTH, N_TILES, 6, D)

    hk = ML_HEADS * ML_DK
    states = []
    new_ckv = new_kpe = None
    for l in range(DEPTH):
        kind, j = l % 3, l // 3
        n1 = norm1_g[l].reshape(1, D)
        if kind == 0:
            w = ml_w_in[j]
            w_gates = w[:, 2 * hk + 2 * D:]
            qkv, o, gg, gt = _ml_in(x, modt[l], n1, w[:, :2 * hk + 2 * D].astype(BF16),
                                    jnp.pad(w_gates, ((0, 0), (0, 128 - 4 * ML_HEADS))).astype(BF16),
                                    w_gates.T.astype(BF16))
            gb = ml_gate_b[j].reshape(4 * ML_HEADS).astype(F32)
            gb_row = jnp.pad(gb, (0, 128 - 4 * ML_HEADS)).reshape(1, 128)
            gb_col = jnp.broadcast_to(gb[:, None], (4 * ML_HEADS, 128))
            hf, hb, st_f, st_b = _ml_core(qkv, gg, gt, gb_row, gb_col,
                                          _ml_init_state(state_mlstm_C, state_mlstm_n, state_mlstm_m, j, 0),
                                          _ml_init_state(state_mlstm_C, state_mlstm_n, state_mlstm_m, j, 1))
            states.append((st_f[:N_CTX_SEQ], st_b[:N_CTX_SEQ]))
            x = _ml_out(hf, hb, o, ml_norm_g[j].reshape(1, D), ml_w_out[j].astype(BF16), x, modt[l])
        elif kind == 1:
            w_in2, w_uq2, w_ukv2 = _mla_weights(mla_w_in[j], mla_w_uq[j], mla_w_ukv[j])
            cos2, sin2 = _rope_tables()
            qn, qr, ckv, kpe = _mla_in(x, modt[l], n1, w_in2, mla_q_norm_g[j].reshape(1, Q_LORA),
                                       mla_kv_norm_g[j].reshape(1, KV_LORA), w_uq2, cos2, sin2)
            new_ckv, new_kpe = ckv[:N_CTX], kpe[:N_CTX]
            kn_tok, v_tok = _kv_expand(ckv, w_ukv2)
            kn_past, v_past = _kv_expand(cache_mla_ckv[:, j].reshape(N_LAT_SEQ * PAST_LEN, KV_LORA), w_ukv2)
            kr_past = cache_mla_kpe[:, j].reshape(N_LAT_SEQ * PAST_LEN, QK_ROPE)
            att = _attention(qn, qr, kn_tok, v_tok, kpe, kn_past, v_past, kr_past)
            x = _res_linear(att, mla_w_out[j].astype(BF16), x, modt[l])
        else:
            x = _fourier(x, modt[l], n1, _dft_tables(), fn_w_out[j].astype(BF16))
        x = _moe(l, x, modt[l], norm2_g[l].reshape(1, D), router_w[l], router_b[l],
                 exp_w1, exp_b1, exp_w2, exp_b2, final_g.reshape(1, D), l == DEPTH - 1)

    y_prompt = x[:N_CTX].reshape(N_CTX_SEQ, CTX_T, D)
    y_sample = x[N_CTX:].reshape(N_LAT_SEQ, LAT_T, D)
    new_mla_ckv = new_ckv.reshape(N_CTX_SEQ, 1, CTX_T, KV_LORA)
    new_mla_kpe = new_kpe.reshape(N_CTX_SEQ, 1, CTX_T, QK_ROPE)
    st = jnp.stack([jnp.stack([sf, sb], axis=1) for sf, sb in states], axis=1)
    new_c = st[..., 0:ML_DV]
    new_n = st[..., ML_N_LANE]
    new_m = st[..., 0, ML_M_LANE]
    return (y_prompt, y_sample, new_mla_ckv, new_mla_kpe, new_c, new_n, new_m)
```

```python
import functools

import numpy as np
import jax
import jax.numpy as jnp
from jax import lax
from jax.experimental import pallas as pl
from jax.experimental.pallas import tpu as pltpu

F32 = jnp.float32
BF16 = jnp.bfloat16

D = 1024
DEPTH = 4
N_CTX_SEQ, CTX_T = 16, 256
N_LAT_SEQ, LAT_T = 2, 2048
N_CTX = N_CTX_SEQ * CTX_T
N_TOK = N_CTX + N_LAT_SEQ * LAT_T
TM = 256
N_TILES = N_TOK // TM
CTX_TILES = N_CTX // TM
LAT_TILES = LAT_T // TM
N_SEQ = N_CTX_SEQ + N_LAT_SEQ
PAST_LEN = 512
GRID_W = 64
RMS_EPS = 1e-6

ML_HEADS, ML_DK, ML_DV = 8, 64, 128
ML_AUG = 256
ML_N_LANE, ML_M_LANE = ML_DV, ML_DV + 1

MLA_HEADS, QK_NOPE, QK_ROPE, V_HEAD = 16, 128, 64, 128
Q_LORA, KV_LORA = 384, 256
ROPE_THETA = 10000.0
ATT_HG = 4
ATT_SCALE = (QK_NOPE + QK_ROPE) ** -0.5

FN_GROUPS, FN_GW = 4, 256

N_EXPERTS, TOP_K = 32, 4
SWIGLU_ALPHA, SWIGLU_LIMIT = 1.702, 7.0
MOE_BM = 256
N_ASSIGN = N_TOK * TOP_K
MOE_NB = N_ASSIGN // MOE_BM + N_EXPERTS
MOE_STEPS = MOE_NB + 1
MOE_SLOTS = 3
MOE_HC = 256
MOE_NHC = D // MOE_HC
MOE_RPC = MOE_BM // MOE_NHC
Y4_ROWS = N_ASSIGN + MOE_SLOTS * MOE_BM

VMEM_LIMIT = 56 * 1024 * 1024


def _cp(n_grid_axes, vmem=None, **kw):
    return pltpu.CompilerParams(dimension_semantics=("arbitrary",) * n_grid_axes, vmem_limit_bytes=vmem, **kw)


def _dot(a, b):
    return jnp.dot(a, b, preferred_element_type=F32)


def _dot_nt(a, b):
    return lax.dot_general(a, b, (((1,), (1,)), ((), ())), preferred_element_type=F32)


def _dot_tn(a, b):
    return lax.dot_general(a, b, (((0,), (0,)), ((), ())), preferred_element_type=F32)


def _sigmoid(x):
    return 1.0 / (1.0 + jnp.exp(-x))


def _log_sigmoid(x):
    return jnp.minimum(x, 0.0) - jnp.log(1.0 + jnp.exp(-jnp.abs(x)))


def _rms(x):
    return x * lax.rsqrt(jnp.mean(x * x, axis=-1, keepdims=True) + RMS_EPS)


def _norm_mod(x, g, shift, scale):
    return (_rms(x) * g) * (1.0 + scale) + shift


def _split2(x):
    hi = x.astype(BF16)
    lo = (x - hi.astype(F32)).astype(BF16)
    return hi, lo


def _split3(x):
    hi = x.astype(BF16)
    r = x - hi.astype(F32)
    mid = r.astype(BF16)
    lo = (r - mid.astype(F32)).astype(BF16)
    return hi, mid, lo


LANE_CHUNKS = D // 128


def _store_token_tiles(ref, row0, val):
    n = val.shape[0]
    for c in range(LANE_CHUNKS):
        ref[pl.ds(row0 * LANE_CHUNKS + c, n, stride=LANE_CHUNKS), :] = val[:, c * 128:(c + 1) * 128]


def _load_token_tile_chunk(ref, row0, n, c):
    return ref[pl.ds(row0 * LANE_CHUNKS + c, n, stride=LANE_CHUNKS), :]


PACK_CHUNKS = LANE_CHUNKS // 2
HI16 = 0xFFFF0000


def _store_packed_rows(ref, val):
    n = val.shape[0]
    for c in range(PACK_CHUNKS):
        lo = pltpu.bitcast(val[:, c * 128:(c + 1) * 128].astype(BF16).astype(F32), jnp.uint32)
        hi = pltpu.bitcast(val[:, D // 2 + c * 128:D // 2 + (c + 1) * 128].astype(BF16).astype(F32), jnp.uint32)
        ref[pl.ds(c, n, stride=PACK_CHUNKS), :] = (hi & jnp.uint32(HI16)) | (lo >> jnp.uint32(16))


def _load_packed_rows(ref, row0, n, out_ref):
    for c in range(PACK_CHUNKS):
        u = ref[pl.ds(row0 * PACK_CHUNKS + c, n, stride=PACK_CHUNKS), :]
        out_ref[:, c * 128:(c + 1) * 128] = pltpu.bitcast(u << jnp.uint32(16), F32).astype(BF16)
        out_ref[:, D // 2 + c * 128:D // 2 + (c + 1) * 128] = pltpu.bitcast(u & jnp.uint32(HI16), F32).astype(BF16)


def _tile_seq(r):
    return jnp.where(r < CTX_TILES, r, CTX_TILES + (r - CTX_TILES) // LAT_TILES)


def _lat_seq(r):
    return jnp.clip((r - CTX_TILES) // LAT_TILES, 0, N_LAT_SEQ - 1)


MOD_TN = 1536


def _mod_kernel(c_ref, w_ref, b_ref, o_ref):
    a = c_ref[...]
    s_hi, s_lo = _split2(a * _sigmoid(a))
    w_hi, w_lo = _split2(w_ref[0])
    o_ref[0] = _dot_split(s_hi, s_lo, w_hi, w_lo) + b_ref[0]


def _modulation(cond8, ada_w, ada_b):
    n_col = ada_w.shape[-1]
    return pl.pallas_call(
        _mod_kernel,
        out_shape=jax.ShapeDtypeStruct((DEPTH, 8, n_col), F32),
        grid=(DEPTH, n_col // MOD_TN),
        in_specs=[pl.BlockSpec((8, D), lambda l, j: (0, 0)),
                  pl.BlockSpec((1, D, MOD_TN), lambda l, j: (l, 0, j)),
                  pl.BlockSpec((1, 1, MOD_TN), lambda l, j: (l, 0, j))],
        out_specs=pl.BlockSpec((1, 8, MOD_TN), lambda l, j: (l, 0, j)),
        compiler_params=_cp(2, VMEM_LIMIT),
    )(cond8, ada_w, ada_b.reshape(DEPTH, 1, n_col))


def _ml_in_kernel(x_ref, mod_ref, g_ref, w_ref, wg_ref, wgt_ref, qkv_ref, o_ref, gg_ref, gt_ref):
    h = _norm_mod(x_ref[...], g_ref[...], mod_ref[0, 0:1, :], mod_ref[0, 1:2, :]).astype(BF16)
    hk = ML_HEADS * ML_DK
    lane = lax.broadcasted_iota(jnp.int32, (1, 2 * hk), 1)
    qscale = jnp.where(lane < hk, ML_DK ** -0.5, 1.0).astype(F32)
    qkv_ref[:, 0:2 * hk] = (_dot(h, w_ref[:, 0:2 * hk]) * qscale).astype(BF16)
    qkv_ref[:, 2 * hk:2 * hk + D] = _dot(h, w_ref[:, 2 * hk:2 * hk + D]).astype(BF16)
    o_ref[...] = _dot(h, w_ref[:, 2 * hk + D:2 * hk + 2 * D])
    gg_ref[...] = _dot(h, wg_ref[...])
    gt_ref[...] = _dot_nt(wgt_ref[...], h)


def _ml_in(x, modt, norm_g, w_main, w_g, w_gt):
    tile = lambda i: (i, 0)
    const = lambda i: (0, 0)
    return pl.pallas_call(
        _ml_in_kernel,
        out_shape=(jax.ShapeDtypeStruct((N_TOK, 2 * D), BF16),
                   jax.ShapeDtypeStruct((N_TOK, D), F32),
                   jax.ShapeDtypeStruct((N_TOK, 128), F32),
                   jax.ShapeDtypeStruct((32, N_TOK), F32)),
        grid=(N_TILES,),
        in_specs=[pl.BlockSpec((TM, D), tile),
                  pl.BlockSpec((1, 6, D), lambda i: (i, 0, 0)),
                  pl.BlockSpec((1, D), const),
                  pl.BlockSpec((D, 3 * D), const),
                  pl.BlockSpec((D, 128), const),
                  pl.BlockSpec((32, D), const)],
        out_specs=(pl.BlockSpec((TM, 2 * D), tile),
                   pl.BlockSpec((TM, D), tile),
                   pl.BlockSpec((TM, 128), tile),
                   pl.BlockSpec((32, TM), lambda i: (0, i))),
        compiler_params=_cp(1, VMEM_LIMIT),
    )(x, modt, norm_g, w_main, w_g, w_gt)


def _ml_direction(lower, q_ref, k_ref, v_ref, g_ref, gt_ref, gbrow_ref, gbcol_ref, h_ref, st_ref, c_s, m_s):
    L = TM
    d_off = 0 if lower else ML_HEADS
    row = lax.broadcasted_iota(jnp.int32, (L, L), 0)
    col = lax.broadcasted_iota(jnp.int32, (L, L), 1)
    mask = (col <= row) if lower else (col >= row)
    tri = jnp.where(mask, 1.0, 0.0).astype(BF16)

    lane = lax.broadcasted_iota(jnp.int32, (L, 128), 1)
    gc = g_ref[...] + gbrow_ref[...]
    gc = jnp.where((lane >= 2 * ML_HEADS) & (lane < 4 * ML_HEADS), _log_sigmoid(gc), gc)
    bc = sum(_dot(tri, p) for p in _split3(gc))
    sub = lax.broadcasted_iota(jnp.int32, (32, L), 0)
    gr = gt_ref[...] + gbcol_ref[:, 0:1]
    gr = jnp.where(sub >= 2 * ML_HEADS, _log_sigmoid(gr), gr)
    br = sum(_dot_nt(p, tri) for p in _split3(gr))

    q = q_ref[...]
    k = k_ref[...]
    v = v_ref[...]
    lane_a = lax.broadcasted_iota(jnp.int32, (L, ML_AUG - ML_DV), 1)
    ones_blk = jnp.where(lane_a == 0, 1.0, 0.0).astype(BF16)
    lane_s = lax.broadcasted_iota(jnp.int32, (ML_DK, ML_AUG), 1)

    for hd in range(ML_HEADS):
        ci = d_off + hd
        cf = 2 * ML_HEADS + d_off + hd
        i_col, b_col = gc[:, ci:ci + 1], bc[:, cf:cf + 1]
        i_row, b_row = gr[ci:ci + 1, :], br[cf:cf + 1, :]
        total = b_col[L - 1:L, :] if lower else b_col[0:1, :]
        m = m_s[hd][0:1, 0:1]
        c_aug = c_s[hd]
        dmat = jnp.where(mask, (b_col - b_row) + i_row, -jnp.inf)
        mi = jnp.max(dmat, axis=1, keepdims=True)
        a_col = b_col + m
        m_row = jnp.maximum(a_col, mi)
        qh = q[:, hd * ML_DK:(hd + 1) * ML_DK]
        kh = k[:, hd * ML_DK:(hd + 1) * ML_DK]
        v_aug = jnp.concatenate([v[:, hd * ML_DV:(hd + 1) * ML_DV], ones_blk], axis=1)
        p = (_dot_nt(qh, kh) * jnp.exp(dmat - mi)).astype(BF16)
        num = jnp.exp(mi - m_row) * _dot(p, v_aug) + jnp.exp(a_col - m_row) * _dot(qh, c_aug.astype(BF16))
        den = jnp.maximum(jnp.abs(num[:, ML_N_LANE:ML_N_LANE + 1]), jnp.exp(-m_row))
        h_ref[:, hd * ML_DV:(hd + 1) * ML_DV] = num[:, 0:ML_DV] / den

        g_col = (total - b_col) + i_col
        m_new = jnp.maximum(total + m, jnp.max(g_col, axis=0, keepdims=True))
        kw = (kh.astype(F32) * jnp.exp(g_col - m_new)).astype(BF16)
        c_new = jnp.exp(total + m - m_new) * c_aug + _dot_tn(kw, v_aug)
        c_s[hd] = c_new
        m_s[hd] = jnp.broadcast_to(m_new, (8, 128))
        st_ref[0, hd] = jnp.where(lane_s == ML_M_LANE, m_new, c_new)


def _ml_core_kernel(qf, kf, vf, gf, gtf, qb, kb, vb, gb, gtb, gbrow, gbcol, initf, initb,
                    hf_ref, hb_ref, stf_ref, stb_ref, cf_s, cb_s, mf_s, mb_s):
    i = pl.program_id(0)
    start_f = (i <= CTX_TILES) | (i == CTX_TILES + LAT_TILES)
    start_b = (i == 0) | (i == LAT_TILES) | (i >= 2 * LAT_TILES)

    def load_state(init_ref, c_s, m_s):
        c_s[...] = init_ref[0]
        for hd in range(ML_HEADS):
            m_s[hd] = jnp.broadcast_to(init_ref[0, hd][0:1, ML_M_LANE:ML_M_LANE + 1], (8, 128))

    @pl.when(start_f)
    def _():
        load_state(initf, cf_s, mf_s)

    @pl.when(start_b)
    def _():
        load_state(initb, cb_s, mb_s)

    _ml_direction(True, qf, kf, vf, gf, gtf, gbrow, gbcol, hf_ref, stf_ref, cf_s, mf_s)
    _ml_direction(False, qb, kb, vb, gb, gtb, gbrow, gbcol, hb_ref, stb_ref, cb_s, mb_s)


def _ml_core(qkv, gg, gt, gb_row, gb_col, init_f, init_b):
    last = N_TILES - 1
    fwd = lambda i: i
    bwd = lambda i: last - i
    hk = ML_HEADS * ML_DK

    def specs(t):
        return [pl.BlockSpec((TM, hk), lambda i: (t(i), 0)),
                pl.BlockSpec((TM, hk), lambda i: (t(i), 1)),
                pl.BlockSpec((TM, D), lambda i: (t(i), 1)),
                pl.BlockSpec((TM, 128), lambda i: (t(i), 0)),
                pl.BlockSpec((32, TM), lambda i: (0, t(i)))]

    def st_spec(t):
        return pl.BlockSpec((1, ML_HEADS, ML_DK, ML_AUG), lambda i: (_tile_seq(t(i)), 0, 0, 0))

    st_shape = jax.ShapeDtypeStruct((N_SEQ, ML_HEADS, ML_DK, ML_AUG), F32)
    return pl.pallas_call(
        _ml_core_kernel,
        out_shape=(jax.ShapeDtypeStruct((N_TOK, D), F32), jax.ShapeDtypeStruct((N_TOK, D), F32),
                   st_shape, st_shape),
        grid=(N_TILES,),
        in_specs=specs(fwd) + specs(bwd) + [
            pl.BlockSpec((1, 128), lambda i: (0, 0)),
            pl.BlockSpec((32, 128), lambda i: (0, 0)),
            st_spec(fwd), st_spec(bwd)],
        out_specs=(pl.BlockSpec((TM, D), lambda i: (i, 0)),
                   pl.BlockSpec((TM, D), lambda i: (last - i, 0)),
                   st_spec(fwd), st_spec(bwd)),
        scratch_shapes=[pltpu.VMEM((ML_HEADS, ML_DK, ML_AUG), F32),
                        pltpu.VMEM((ML_HEADS, ML_DK, ML_AUG), F32),
                        pltpu.VMEM((ML_HEADS, 8, 128), F32),
                        pltpu.VMEM((ML_HEADS, 8, 128), F32)],
        compiler_params=_cp(1, VMEM_LIMIT),
    )(qkv, qkv, qkv, gg, gt, qkv, qkv, qkv, gg, gt, gb_row, gb_col, init_f, init_b)


def _ml_out_kernel(hf_ref, hb_ref, o_ref, ng_ref, w_ref, x_ref, mod_ref, out_ref, z_s):
    hh = hf_ref[...] + hb_ref[...]
    o = o_ref[...]
    for hd in range(ML_HEADS):
        sl = slice(hd * ML_DV, (hd + 1) * ML_DV)
        z_s[:, sl] = (_sigmoid(o[:, sl]) * (_rms(hh[:, sl]) * ng_ref[:, sl])).astype(BF16)
    out_ref[...] = x_ref[...] + mod_ref[0, 2:3, :] * _dot(z_s[...], w_ref[...])


def _ml_out(hf, hb, o, norm_g, w_out, x, modt):
    tile = lambda i: (i, 0)
    const = lambda i: (0, 0)
    return pl.pallas_call(
        _ml_out_kernel,
        out_shape=jax.ShapeDtypeStruct((N_TOK, D), F32),
        grid=(N_TILES,),
        in_specs=[pl.BlockSpec((TM, D), tile), pl.BlockSpec((TM, D), tile), pl.BlockSpec((TM, D), tile),
                  pl.BlockSpec((1, D), const), pl.BlockSpec((D, D), const),
                  pl.BlockSpec((TM, D), tile), pl.BlockSpec((1, 6, D), lambda i: (i, 0, 0))],
        out_specs=pl.BlockSpec((TM, D), tile),
        scratch_shapes=[pltpu.VMEM((TM, D), BF16)],
        compiler_params=_cp(1, VMEM_LIMIT),
    )(hf, hb, o, norm_g, w_out, x, modt)


MLA_IN_COLS = 896
NOPE_ALL = MLA_HEADS * QK_NOPE
ROPE_ALL = MLA_HEADS * QK_ROPE


def _mla_in_kernel(x_ref, mod_ref, g_ref, w_ref, qg_ref, kvg_ref, wuq_ref, cos_ref, sin_ref,
                   qn_ref, qr_ref, ckv_ref, kpe_ref):
    h = _norm_mod(x_ref[...], g_ref[...], mod_ref[0, 0:1, :], mod_ref[0, 1:2, :]).astype(BF16)
    c = _dot(h, w_ref[...])
    cos, sin = cos_ref[...], sin_ref[...]
    ckv_ref[...] = _rms(c[:, Q_LORA:Q_LORA + KV_LORA]) * kvg_ref[...]
    kpe_ref[...] = c[:, 640:704] * cos[:, 0:QK_ROPE] + c[:, 768:832] * sin[:, 0:QK_ROPE]
    cq = (_rms(c[:, 0:Q_LORA]) * qg_ref[...]).astype(BF16)
    qn_ref[...] = _dot(cq, wuq_ref[:, 0:NOPE_ALL]).astype(BF16)
    rope = _dot(cq, wuq_ref[:, NOPE_ALL:NOPE_ALL + ROPE_ALL])
    swapped = _dot(cq, wuq_ref[:, NOPE_ALL + ROPE_ALL:NOPE_ALL + 2 * ROPE_ALL])
    for s in range(ROPE_ALL // 128):
        sl = slice(s * 128, (s + 1) * 128)
        qr_ref[:, sl] = (rope[:, sl] * cos + swapped[:, sl] * sin).astype(BF16)


def _mla_in(x, modt, norm_g, w_in2, qg, kvg, w_uq2, cos2, sin2):
    tile = lambda i: (i, 0)
    const = lambda i: (0, 0)
    return pl.pallas_call(
        _mla_in_kernel,
        out_shape=(jax.ShapeDtypeStruct((N_TOK, NOPE_ALL), BF16),
                   jax.ShapeDtypeStruct((N_TOK, ROPE_ALL), BF16),
                   jax.ShapeDtypeStruct((N_TOK, KV_LORA), F32),
                   jax.ShapeDtypeStruct((N_TOK, QK_ROPE), F32)),
        grid=(N_TILES,),
        in_specs=[pl.BlockSpec((TM, D), tile), pl.BlockSpec((1, 6, D), lambda i: (i, 0, 0)),
                  pl.BlockSpec((1, D), const), pl.BlockSpec((D, MLA_IN_COLS), const),
                  pl.BlockSpec((1, Q_LORA), const), pl.BlockSpec((1, KV_LORA), const),
                  pl.BlockSpec((Q_LORA, NOPE_ALL + 2 * ROPE_ALL), const),
                  pl.BlockSpec((TM, 128), tile), pl.BlockSpec((TM, 128), tile)],
        out_specs=(pl.BlockSpec((TM, NOPE_ALL), tile), pl.BlockSpec((TM, ROPE_ALL), tile),
                   pl.BlockSpec((TM, KV_LORA), tile), pl.BlockSpec((TM, QK_ROPE), tile)),
        compiler_params=_cp(1, VMEM_LIMIT),
    )(x, modt, norm_g, w_in2, qg, kvg, w_uq2, cos2, sin2)


def _kv_expand_kernel(c_ref, w_ref, kn_ref, v_ref):
    c = c_ref[...].astype(BF16)
    kn_ref[...] = _dot(c, w_ref[:, 0:NOPE_ALL]).astype(BF16)
    v_ref[...] = _dot(c, w_ref[:, NOPE_ALL:2 * NOPE_ALL]).astype(BF16)


def _kv_expand(ckv, w_ukv2):
    rows = ckv.shape[0]
    tile = lambda i: (i, 0)
    shp = jax.ShapeDtypeStruct((rows, NOPE_ALL), BF16)
    return pl.pallas_call(
        _kv_expand_kernel,
        out_shape=(shp, shp),
        grid=(rows // TM,),
        in_specs=[pl.BlockSpec((TM, KV_LORA), tile), pl.BlockSpec((KV_LORA, 2 * NOPE_ALL), lambda i: (0, 0))],
        out_specs=(pl.BlockSpec((TM, NOPE_ALL), tile), pl.BlockSpec((TM, NOPE_ALL), tile)),
        compiler_params=_cp(1, VMEM_LIMIT),
    )(ckv, w_ukv2)


def _attend(qn_ref, qr_ref, segs, out_ref):
    kr = [s[2][...].astype(BF16) for s in segs]
    for hd in range(ATT_HG):
        sl = slice(hd * QK_NOPE, (hd + 1) * QK_NOPE)
        qn = qn_ref[:, sl]
        qr = qr_ref[:, hd * QK_ROPE:(hd + 1) * QK_ROPE]
        scores = [(_dot_nt(qn, s[0][:, sl]) + _dot_nt(qr, kr_s)) * ATT_SCALE for s, kr_s in zip(segs, kr)]
        m = functools.reduce(jnp.maximum, [jnp.max(s, axis=1, keepdims=True) for s in scores])
        ps = [jnp.exp(s - m) for s in scores]
        denom = sum(jnp.sum(p, axis=1, keepdims=True) for p in ps)
        acc = sum(_dot(p.astype(BF16), s[1][:, sl]) for p, s in zip(ps, segs))
        out_ref[:, sl] = (acc / denom).astype(BF16)


def _attn_kernel(qn_ref, qr_ref, knc_ref, vc_ref, krc_ref, knl_ref, vl_ref, krl_ref, knp_ref, vp_ref, krp_ref,
                 out_ref):
    r = pl.program_id(1)

    @pl.when(r < CTX_TILES)
    def _():
        _attend(qn_ref, qr_ref, [(knc_ref, vc_ref, krc_ref)], out_ref)

    @pl.when(r >= CTX_TILES)
    def _():
        _attend(qn_ref, qr_ref, [(knp_ref, vp_ref, krp_ref), (knl_ref, vl_ref, krl_ref)], out_ref)


def _attention(qn, qr, kn_tok, v_tok, kr_tok, kn_past, v_past, kr_past):
    hw = ATT_HG * QK_NOPE
    ctx_t = lambda g, r: (jnp.minimum(r, CTX_TILES - 1), g)
    lat_t = lambda g, r: (N_CTX // LAT_T + _lat_seq(r), g)
    past_t = lambda g, r: (_lat_seq(r), g)
    col0 = lambda f: (lambda g, r: (f(g, r)[0], 0))
    return pl.pallas_call(
        _attn_kernel,
        out_shape=jax.ShapeDtypeStruct((N_TOK, NOPE_ALL), BF16),
        grid=(MLA_HEADS // ATT_HG, N_TILES),
        in_specs=[pl.BlockSpec((TM, hw), lambda g, r: (r, g)),
                  pl.BlockSpec((TM, ATT_HG * QK_ROPE), lambda g, r: (r, g)),
                  pl.BlockSpec((CTX_T, hw), ctx_t), pl.BlockSpec((CTX_T, hw), ctx_t),
                  pl.BlockSpec((CTX_T, QK_ROPE), col0(ctx_t)),
                  pl.BlockSpec((LAT_T, hw), lat_t), pl.BlockSpec((LAT_T, hw), lat_t),
                  pl.BlockSpec((LAT_T, QK_ROPE), col0(lat_t)),
                  pl.BlockSpec((PAST_LEN, hw), past_t), pl.BlockSpec((PAST_LEN, hw), past_t),
                  pl.BlockSpec((PAST_LEN, QK_ROPE), col0(past_t))],
        out_specs=pl.BlockSpec((TM, hw), lambda g, r: (r, g)),
        compiler_params=_cp(2, VMEM_LIMIT),
    )(qn, qr, kn_tok, v_tok, kr_tok, kn_tok, v_tok, kr_tok, kn_past, v_past, kr_past)


def _res_linear_kernel(a_ref, w_ref, x_ref, mod_ref, out_ref):
    out_ref[...] = x_ref[...] + mod_ref[0, 2:3, :] * _dot(a_ref[...], w_ref[...])


def _res_linear(a, w, x, modt):
    k = a.shape[1]
    tile = lambda i: (i, 0)
    return pl.pallas_call(
        _res_linear_kernel,
        out_shape=jax.ShapeDtypeStruct((N_TOK, D), F32),
        grid=(N_TILES,),
        in_specs=[pl.BlockSpec((TM, k), tile), pl.BlockSpec((k, D), lambda i: (0, 0)),
                  pl.BlockSpec((TM, D), tile), pl.BlockSpec((1, 6, D), lambda i: (i, 0, 0))],
        out_specs=pl.BlockSpec((TM, D), tile),
        compiler_params=_cp(1, VMEM_LIMIT),
    )(a, w, x, modt)


def _dot_split(a_hi, a_lo, b_hi, b_lo):
    return _dot(a_hi, b_hi) + (_dot(a_hi, b_lo) + _dot(a_lo, b_hi))


def _fn_channel_dft(x, g, shift, scale, wc_hi, wc_lo):
    h_hi, h_lo = _split2(_norm_mod(x, g, shift, scale))
    a_parts, b_parts = [], []
    for grp in range(FN_GROUPS):
        sl = slice(grp * FN_GW, (grp + 1) * FN_GW)
        ab = _dot_split(h_hi[:, sl], h_lo[:, sl], wc_hi, wc_lo)
        a_parts.append(ab[:, 0:FN_GW])
        b_parts.append(ab[:, FN_GW:2 * FN_GW])
    return jnp.concatenate(a_parts, axis=1), jnp.concatenate(b_parts, axis=1)


FN_STEPS = CTX_TILES + N_LAT_SEQ * 2 * LAT_TILES


def _fn_step(t):
    u = jnp.maximum(t - CTX_TILES, 0)
    seq, ph, tile = u // (2 * LAT_TILES), (u % (2 * LAT_TILES)) // LAT_TILES, u % LAT_TILES
    is_ctx = t < CTX_TILES
    return is_ctx, ph, tile, jnp.where(is_ctx, t, CTX_TILES + seq * LAT_TILES + tile)


def _fn_kernel(x_ref, mod_ref, g_ref, wc_ref, tc_ref, tl_ref, w_ref, out_ref, ab_hi_s, ab_lo_s):
    is_ctx, ph, tile, _ = _fn_step(pl.program_id(0))
    g = g_ref[...]
    shift, scale, gate = mod_ref[0, 0:1, :], mod_ref[0, 1:2, :], mod_ref[0, 2:3, :]
    wc_hi, wc_lo = _split2(wc_ref[...])

    def finish(f, t_len):
        f = f * ((t_len * FN_GW) ** -0.5)
        out_ref[...] = x_ref[...] + gate * _dot(f.astype(BF16), w_ref[...])

    @pl.when(is_ctx)
    def _():
        a, b = _fn_channel_dft(x_ref[...], g, shift, scale, wc_hi, wc_lo)
        ab_hi, ab_lo = _split2(jnp.concatenate([a, b], axis=0))
        finish(_dot_split(*_split2(tc_ref[...]), ab_hi, ab_lo), CTX_T)

    @pl.when(jnp.logical_not(is_ctx) & (ph == 0))
    def _():
        a, b = _fn_channel_dft(x_ref[...], g, shift, scale, wc_hi, wc_lo)
        a_hi, a_lo = _split2(a)
        b_hi, b_lo = _split2(b)
        rows = pl.ds(pl.multiple_of(tile * TM, TM), TM)
        rows_b = pl.ds(pl.multiple_of(LAT_T + tile * TM, TM), TM)
        ab_hi_s[rows, :] = a_hi
        ab_lo_s[rows, :] = a_lo
        ab_hi_s[rows_b, :] = b_hi
        ab_lo_s[rows_b, :] = b_lo

    @pl.when(jnp.logical_not(is_ctx) & (ph == 1))
    def _():
        finish(_dot_split(*_split2(tl_ref[...]), ab_hi_s[...], ab_lo_s[...]), LAT_T)


def _fourier(x, modt, norm_g, tabs, w_out):
    wc, tc, tl = tabs
    const = lambda t: (0, 0)
    x_tile = lambda t: (_fn_step(t)[3], 0)

    def out_tile(t):
        is_ctx, ph, tile, gt = _fn_step(t)
        return (jnp.where(is_ctx | (ph == 1), gt, gt - tile), 0)

    def tab_row(t):
        is_ctx, ph, tile, _ = _fn_step(t)
        return (jnp.where(is_ctx | (ph == 0), 0, tile), 0)

    return pl.pallas_call(
        _fn_kernel,
        out_shape=jax.ShapeDtypeStruct((N_TOK, D), F32),
        grid=(FN_STEPS,),
        in_specs=[pl.BlockSpec((TM, D), x_tile),
                  pl.BlockSpec((1, 6, D), lambda t: (_fn_step(t)[3], 0, 0)),
                  pl.BlockSpec((1, D), const),
                  pl.BlockSpec((FN_GW, 2 * FN_GW), const),
                  pl.BlockSpec((CTX_T, 2 * CTX_T), const),
                  pl.BlockSpec((TM, 2 * LAT_T), tab_row),
                  pl.BlockSpec((D, D), const)],
        out_specs=pl.BlockSpec((TM, D), out_tile),
        scratch_shapes=[pltpu.VMEM((2 * LAT_T, D), BF16), pltpu.VMEM((2 * LAT_T, D), BF16)],
        compiler_params=_cp(1, VMEM_LIMIT),
    )(x, modt, norm_g, wc, tc, tl, w_out)


@functools.lru_cache(maxsize=None)
def _dft_tables():
    def cos_sin(n):
        k = np.arange(n, dtype=np.int64)
        ang = ((k[:, None] * k[None, :]) % n).astype(np.float64) * (2.0 * np.pi / n)
        return np.cos(ang), np.sin(ang)

    cc, sc = cos_sin(FN_GW)
    out = [np.concatenate([cc, sc], axis=1).astype(np.float32)]
    for t_len in (CTX_T, LAT_T):
        ct, st = cos_sin(t_len)
        out.append(np.concatenate([ct, -st], axis=1).astype(np.float32))
    return tuple(out)


def _router_kernel(x_ref, mod_ref, g_ref, rw_hi_ref, rw_lo_ref, rb_ref, h_ref, idx_ref, gate_ref, cnt_ref):
    h = _norm_mod(x_ref[...], g_ref[...], mod_ref[0, 3:4, :], mod_ref[0, 4:5, :])
    _store_packed_rows(h_ref, h)
    h_hi, h_lo = _split2(h)
    logits = _dot_split(h_hi, h_lo, rw_hi_ref[...], rw_lo_ref[...]) + rb_ref[...]
    lane = lax.broadcasted_iota(jnp.int32, logits.shape, 1)
    lane_f = lane.astype(F32)
    cur = jnp.where(lane < N_EXPERTS, logits, -jnp.inf)
    vals, idxs = [], []
    for _ in range(TOP_K):
        m = jnp.max(cur, axis=1, keepdims=True)
        ik = jnp.min(jnp.where(cur == m, lane_f, 128.0), axis=1, keepdims=True).astype(jnp.int32)
        vals.append(m)
        idxs.append(ik)
        cur = jnp.where(lane == ik, -jnp.inf, cur)
    es = [jnp.exp(v - vals[0]) for v in vals]
    denom = functools.reduce(lambda a, b: a + b, es)
    idx_out = jnp.zeros(logits.shape, jnp.int32)
    gate_out = jnp.zeros(logits.shape, F32)
    for kk in range(TOP_K):
        idx_out = jnp.where(lane == kk, idxs[kk], idx_out)
        gate_out = jnp.where(lane == kk, es[kk] / denom, gate_out)
    idx_ref[...] = idx_out
    gate_ref[...] = gate_out

    @pl.when(pl.program_id(0) == 0)
    def _():
        cnt_ref[...] = jnp.zeros(cnt_ref.shape, F32)

    hits = functools.reduce(lambda a, b: a + b, [jnp.where(lane == ik, 1.0, 0.0) for ik in idxs])
    cnt_ref[...] += jnp.sum(hits, axis=0, keepdims=True)


def _router(x, modt, norm_g, rw_hi, rw_lo, rb):
    tile = lambda i: (i, 0)
    const = lambda i: (0, 0)
    return pl.pallas_call(
        _router_kernel,
        out_shape=(jax.ShapeDtypeStruct((N_TOK * PACK_CHUNKS, 128), jnp.uint32),
                   jax.ShapeDtypeStruct((N_TOK, 128), jnp.int32),
                   jax.ShapeDtypeStruct((N_TOK, 128), F32),
                   jax.ShapeDtypeStruct((1, 128), F32)),
        grid=(N_TILES,),
        in_specs=[pl.BlockSpec((TM, D), tile), pl.BlockSpec((1, 6, D), lambda i: (i, 0, 0)),
                  pl.BlockSpec((1, D), const), pl.BlockSpec((D, 128), const), pl.BlockSpec((D, 128), const),
                  pl.BlockSpec((1, 128), const)],
        out_specs=(pl.BlockSpec((TM * PACK_CHUNKS, 128), tile), pl.BlockSpec((TM, 128), tile),
                   pl.BlockSpec((TM, 128), tile), pl.BlockSpec((1, 128), const)),
        compiler_params=_cp(1, VMEM_LIMIT),
    )(x, modt, norm_g, rw_hi, rw_lo, rb)


def _route_plan(idx, cnt):
    ids = idx[:, 0:TOP_K].reshape(-1)
    order = jnp.argsort(ids, stable=True).astype(jnp.int32)
    counts = cnt[0, 0:N_EXPERTS].astype(jnp.int32)
    starts = jnp.cumsum(counts) - counts
    nblk = (counts + MOE_BM - 1) // MOE_BM
    pb_end = jnp.cumsum(nblk)
    pb_start = pb_end - nblk
    total = pb_end[-1]
    b = jnp.arange(MOE_STEPS, dtype=jnp.int32)
    be = jnp.minimum(jnp.sum((pb_end[None, :] <= b[:, None]).astype(jnp.int32), axis=1), N_EXPERTS - 1)
    be = jnp.where(b < total, be, be[jnp.maximum(total - 1, 0)])
    local = b - pb_start[be]
    base = starts[be] + local * MOE_BM
    nvalid = jnp.clip(counts[be] - local * MOE_BM, 0, MOE_BM)
    first = ((local == 0) & (b < total)).astype(jnp.int32)
    return (be.astype(jnp.int32), first, base.astype(jnp.int32), nvalid.astype(jnp.int32),
            total.reshape(1).astype(jnp.int32), order)


def _expert_kernel(be_ref, first_ref, base_ref, nv_ref, total_ref, order_ref,
                   x_hbm, w1_ref, b1_ref, w2_ref, b2_ref, y4_hbm,
                   xs_buf, y_buf, y_acc, xs_bf, w1b, w2b, gsem, ssem):
    b = pl.program_id(0)
    total = total_ref[0]

    def tile_rows(r, per_row=LANE_CHUNKS):
        return pl.ds(pl.multiple_of(r * per_row, per_row), per_row)

    def slot_rows(slot, per_row=LANE_CHUNKS):
        return pl.ds(pl.multiple_of(slot * (MOE_BM * per_row), MOE_BM * per_row), MOE_BM * per_row)

    def gather_start(base, nv, slot, row0, i):
        a = order_ref[base + jnp.minimum(row0 + i, nv - 1)]
        pltpu.make_async_copy(x_hbm.at[tile_rows(a >> 2, PACK_CHUNKS), :],
                              xs_buf.at[tile_rows(slot * MOE_BM + row0 + i, PACK_CHUNKS), :], gsem.at[slot]).start()

    def scatter_start(base, nv, slot, row0, i):
        row = row0 + i
        a = order_ref[base + jnp.maximum(jnp.minimum(row, nv - 1), 0)]
        dst = jnp.where(row < nv, (a & (TOP_K - 1)) * N_TOK + (a >> 2), N_ASSIGN + slot * MOE_BM + row)
        pltpu.make_async_copy(y_buf.at[tile_rows(slot * MOE_BM + row), :], y4_hbm.at[tile_rows(dst), :],
                              ssem.at[slot]).start(priority=1)

    def wait_gather(slot):
        pltpu.make_async_copy(x_hbm.at[slot_rows(0, PACK_CHUNKS), :], xs_buf.at[slot_rows(slot, PACK_CHUNKS), :],
                              gsem.at[slot]).wait()

    def wait_scatter(slot):
        pltpu.make_async_copy(y_buf.at[slot_rows(slot), :], y4_hbm.at[slot_rows(0), :], ssem.at[slot]).wait()

    def row_loop(fn):
        def body(g, c):
            for i in range(8):
                fn(g * 8, i)
            return c
        lax.fori_loop(0, MOE_BM // 8, body, 0)

    @pl.when(b == 0)
    def _():
        y_buf[...] = jnp.zeros(y_buf.shape, F32)
        for s in range(2):
            pltpu.make_async_copy(y_buf.at[slot_rows(s), :], y4_hbm.at[slot_rows(N_ASSIGN // MOE_BM + s), :],
                                  ssem.at[s]).start()
        row_loop(functools.partial(gather_start, base_ref[0], nv_ref[0], 0))
        blk1 = jnp.minimum(1, total - 1)
        row_loop(functools.partial(gather_start, base_ref[blk1], nv_ref[blk1], 1))

    @pl.when(b < total)
    def _():
        slot = b % MOE_SLOTS

        @pl.when(first_ref[b] == 1)
        def _():
            for j in range(2 * MOE_NHC):
                w1b[j] = w1_ref[0, 0, :, j * MOE_HC:(j + 1) * MOE_HC].astype(BF16)
            for j in range(MOE_NHC):
                w2b[j] = w2_ref[0, 0, j * MOE_HC:(j + 1) * MOE_HC, :].astype(BF16)

        wait_gather(slot)
        wait_scatter(slot)
        _load_packed_rows(xs_buf, slot * MOE_BM, MOE_BM, xs_bf)
        y_acc[...] = jnp.broadcast_to(b2_ref[0, 0], (MOE_BM, D))

        nblk = jnp.minimum(b + 2, total - 1)
        nbase, nnv = base_ref[nblk], nv_ref[nblk]
        pblk = jnp.maximum(b - 1, 0)
        pbase, pnv = base_ref[pblk], jnp.where(b >= 1, nv_ref[pblk], 0)
        oslot = (b + 2) % MOE_SLOTS

        def hidden_chunk(j, c):
            xb = xs_bf[...]
            hg = _dot(xb, w1b[j]) + b1_ref[0, 0, pl.ds(j, 1), :]
            hu = _dot(xb, w1b[MOE_NHC + j]) + b1_ref[0, 0, pl.ds(MOE_NHC + j, 1), :]
            gate = jnp.minimum(hg, SWIGLU_LIMIT)
            up = jnp.clip(hu, -SWIGLU_LIMIT, SWIGLU_LIMIT)
            act = ((up + 1.0) * (gate * _sigmoid(SWIGLU_ALPHA * gate))).astype(BF16)
            y_acc[...] += _dot(act, w2b[j])
            for i in range(MOE_RPC):
                gather_start(nbase, nnv, oslot, j * MOE_RPC, i)
                scatter_start(pbase, pnv, oslot, j * MOE_RPC, i)
            return c
        lax.fori_loop(0, MOE_NHC, hidden_chunk, 0)
        _store_token_tiles(y_buf, slot * MOE_BM, y_acc[...])

    @pl.when(b == total)
    def _():
        last = total - 1
        row_loop(functools.partial(scatter_start, base_ref[last], nv_ref[last], last % MOE_SLOTS))

    @pl.when(b == MOE_STEPS - 1)
    def _():
        for s in range(MOE_SLOTS):
            wait_scatter(s)
        wait_gather(total % MOE_SLOTS)
        wait_gather((total + 1) % MOE_SLOTS)


def _experts(layer, plan, h2, w1, b1, w2, b2):
    be, first, base, nvalid, total, order = plan
    wmap = lambda b, be, *_: (layer, be[b], 0, 0)
    grid_spec = pltpu.PrefetchScalarGridSpec(
        num_scalar_prefetch=6,
        grid=(MOE_STEPS,),
        in_specs=[pl.BlockSpec(memory_space=pl.ANY),
                  pl.BlockSpec((1, 1, D, 2 * D), wmap), pl.BlockSpec((1, 1, 2 * MOE_NHC, MOE_HC), wmap),
                  pl.BlockSpec((1, 1, D, D), wmap), pl.BlockSpec((1, 1, 1, D), wmap)],
        out_specs=pl.BlockSpec(memory_space=pl.ANY),
        scratch_shapes=[pltpu.VMEM((MOE_SLOTS * MOE_BM * PACK_CHUNKS, 128), jnp.uint32),
                        pltpu.VMEM((MOE_SLOTS * MOE_BM * LANE_CHUNKS, 128), F32),
                        pltpu.VMEM((MOE_BM, D), F32), pltpu.VMEM((MOE_BM, D), BF16),
                        pltpu.VMEM((2 * MOE_NHC, D, MOE_HC), BF16), pltpu.VMEM((MOE_NHC, MOE_HC, D), BF16),
                        pltpu.SemaphoreType.DMA((MOE_SLOTS,)), pltpu.SemaphoreType.DMA((MOE_SLOTS,))],
    )
    return pl.pallas_call(
        _expert_kernel,
        out_shape=jax.ShapeDtypeStruct((Y4_ROWS * LANE_CHUNKS, 128), F32),
        grid_spec=grid_spec,
        compiler_params=_cp(1, VMEM_LIMIT, disable_bounds_checks=True),
    )(be, first, base, nvalid, total, order, h2, w1, b1.reshape(DEPTH, N_EXPERTS, 2 * MOE_NHC, MOE_HC), w2,
      b2.reshape(DEPTH, N_EXPERTS, 1, D))


def _combine_kernel(final, x_ref, y0, y1, y2, y3, gate_ref, mod_ref, fg_ref, out_ref):
    gates = gate_ref[...]
    parts = []
    for c in range(LANE_CHUNKS):
        moe = gates[:, 0:1] * _load_token_tile_chunk(y0, 0, TM, c)
        for kk, y in ((1, y1), (2, y2), (3, y3)):
            moe = moe + gates[:, kk:kk + 1] * _load_token_tile_chunk(y, 0, TM, c)
        parts.append(moe)
    x = x_ref[...] + mod_ref[0, 5:6, :] * jnp.concatenate(parts, axis=1)
    out_ref[...] = _rms(x) * fg_ref[...] if final else x


def _combine(x, y4, gate, modt, final_g, final):
    tile = lambda i: (i, 0)
    plane = lambda kk: pl.BlockSpec((TM * LANE_CHUNKS, 128), lambda i: (kk * N_TILES + i, 0))
    return pl.pallas_call(
        functools.partial(_combine_kernel, final),
        out_shape=jax.ShapeDtypeStruct((N_TOK, D), F32),
        grid=(N_TILES,),
        in_specs=[pl.BlockSpec((TM, D), tile), plane(0), plane(1), plane(2), plane(3),
                  pl.BlockSpec((TM, 128), tile), pl.BlockSpec((1, 6, D), lambda i: (i, 0, 0)),
                  pl.BlockSpec((1, D), lambda i: (0, 0))],
        out_specs=pl.BlockSpec((TM, D), tile),
        compiler_params=_cp(1, VMEM_LIMIT),
    )(x, y4, y4, y4, y4, gate, modt, final_g)


def _moe(layer, x, modt, norm2_g, router_w, router_b, w1, b1, w2, b2, final_g, final):
    rw = jnp.pad(router_w, ((0, 0), (0, 128 - N_EXPERTS)))
    rw_hi, rw_lo = _split2(rw)
    rb = jnp.pad(router_b, (0, 128 - N_EXPERTS)).reshape(1, 128)
    h2, idx, gate, cnt = _router(x, modt, norm2_g, rw_hi, rw_lo, rb)
    y4 = _experts(layer, _route_plan(idx, cnt), h2, w1, b1, w2, b2)
    return _combine(x, y4, gate, modt, final_g, final)


def _rope_tables():
    def tables(n_tokens):
        rows = n_tokens // GRID_W
        row = jnp.repeat(jnp.arange(rows, dtype=F32), GRID_W)
        col = jnp.tile(jnp.arange(GRID_W, dtype=F32), rows)
        n_freq = QK_ROPE // 4
        inv = ROPE_THETA ** (-jnp.arange(n_freq, dtype=F32) / n_freq)
        ang = jnp.stack([row[:, None] * inv, col[:, None] * inv], axis=1)
        return jnp.cos(ang), jnp.sin(ang)

    cos, sin = tables(LAT_T)
    cos64 = jnp.concatenate([cos[:, 0], cos[:, 0], cos[:, 1], cos[:, 1]], axis=1)
    sin64 = jnp.concatenate([-sin[:, 0], sin[:, 0], -sin[:, 1], sin[:, 1]], axis=1)
    cos_all = jnp.concatenate([jnp.ones((N_CTX, QK_ROPE), F32)] + [cos64] * N_LAT_SEQ, axis=0)
    sin_all = jnp.concatenate([jnp.zeros((N_CTX, QK_ROPE), F32)] + [sin64] * N_LAT_SEQ, axis=0)
    return jnp.tile(cos_all, (1, 2)), jnp.tile(sin_all, (1, 2))


_PAIR_SWAP = np.concatenate([np.arange(16, 32), np.arange(0, 16), np.arange(48, 64), np.arange(32, 48)])


def _mla_weights(w_in, w_uq, w_ukv):
    kpe = w_in[:, Q_LORA + KV_LORA:]
    pad = jnp.zeros((D, 64), F32)
    w_in2 = jnp.concatenate([w_in[:, :Q_LORA + KV_LORA], kpe, pad, kpe[:, _PAIR_SWAP], pad], axis=1)
    uq = w_uq.reshape(Q_LORA, MLA_HEADS, QK_NOPE + QK_ROPE)
    rope = uq[:, :, QK_NOPE:]
    w_uq2 = jnp.concatenate([uq[:, :, :QK_NOPE].reshape(Q_LORA, NOPE_ALL), rope.reshape(Q_LORA, ROPE_ALL),
                             rope[:, :, _PAIR_SWAP].reshape(Q_LORA, ROPE_ALL)], axis=1)
    ukv = w_ukv.reshape(KV_LORA, MLA_HEADS, QK_NOPE + V_HEAD)
    w_ukv2 = jnp.concatenate([ukv[:, :, :QK_NOPE].reshape(KV_LORA, NOPE_ALL),
                              ukv[:, :, QK_NOPE:].reshape(KV_LORA, NOPE_ALL)], axis=1)
    return w_in2.astype(BF16), w_uq2.astype(BF16), w_ukv2.astype(BF16)


def _ml_init_state(state_c, state_n, state_m, j, direction):
    c = state_c[:, j, direction].astype(F32)
    n = state_n[:, j, direction].astype(F32)[..., None]
    m = jnp.broadcast_to(state_m[:, j, direction].astype(F32)[..., None, None], n.shape)
    pad = jnp.zeros(c.shape[:-1] + (ML_AUG - ML_DV - 2,), F32)
    lat = jnp.concatenate([c, n, m, pad], axis=-1)
    return jnp.concatenate([jnp.zeros((N_CTX_SEQ,) + lat.shape[1:], F32), lat], axis=0)


def kernel(x_prompt, x_sample, cache_mla_ckv, cache_mla_kpe, state_mlstm_C, state_mlstm_n, state_mlstm_m, c, c_ctx, norm1_g, norm2_g, ada_w, ada_b, ml_w_in, ml_gate_b, ml_norm_g, ml_w_out, mla_w_in, mla_q_norm_g, mla_w_uq, mla_kv_norm_g, mla_w_ukv, mla_w_out, fn_w_out, router_w, router_b, exp_w1, exp_b1, exp_w2, exp_b2, final_g):
    x = jnp.concatenate([x_prompt.reshape(N_CTX, D), x_sample.reshape(N_LAT_SEQ * LAT_T, D)], axis=0)

    cond8 = jnp.concatenate([c_ctx[None, :], c, jnp.zeros((8 - 1 - N_LAT_SEQ, D), F32)], axis=0)
    mod = _modulation(cond8, ada_w, ada_b)
    tile_cond = np.concatenate([np.zeros(CTX_TILES, np.int32)] +
                               [np.full(LAT_TILES, 1 + s, np.int32) for s in range(N_LAT_SEQ)])
    modt = mod[:, tile_cond].reshape(DEPTH, N_TILES, 6, D)

    hk = ML_HEADS * ML_DK
    states = []
    new_ckv = new_kpe = None
    for l in range(DEPTH):
        kind, j = l % 3, l // 3
        n1 = norm1_g[l].reshape(1, D)
        if kind == 0:
            w = ml_w_in[j]
            w_gates = w[:, 2 * hk + 2 * D:]
            qkv, o, gg, gt = _ml_in(x, modt[l], n1, w[:, :2 * hk + 2 * D].astype(BF16),
                                    jnp.pad(w_gates, ((0, 0), (0, 128 - 4 * ML_HEADS))).astype(BF16),
                                    w_gates.T.astype(BF16))
            gb = ml_gate_b[j].reshape(4 * ML_HEADS).astype(F32)
            gb_row = jnp.pad(gb, (0, 128 - 4 * ML_HEADS)).reshape(1, 128)
            gb_col = jnp.broadcast_to(gb[:, None], (4 * ML_HEADS, 128))
            hf, hb, st_f, st_b = _ml_core(qkv, gg, gt, gb_row, gb_col,
                                          _ml_init_state(state_mlstm_C, state_mlstm_n, state_mlstm_m, j, 0),
                                          _ml_init_state(state_mlstm_C, state_mlstm_n, state_mlstm_m, j, 1))
            states.append((st_f[:N_CTX_SEQ], st_b[:N_CTX_SEQ]))
            x = _ml_out(hf, hb, o, ml_norm_g[j].reshape(1, D), ml_w_out[j].astype(BF16), x, modt[l])
        elif kind == 1:
            w_in2, w_uq2, w_ukv2 = _mla_weights(mla_w_in[j], mla_w_uq[j], mla_w_ukv[j])
            cos2, sin2 = _rope_tables()
            qn, qr, ckv, kpe = _mla_in(x, modt[l], n1, w_in2, mla_q_norm_g[j].reshape(1, Q_LORA),
                                       mla_kv_norm_g[j].reshape(1, KV_LORA), w_uq2, cos2, sin2)
            new_ckv, new_kpe = ckv[:N_CTX], kpe[:N_CTX]
            kn_tok, v_tok = _kv_expand(ckv, w_ukv2)
            kn_past, v_past = _kv_expand(cache_mla_ckv[:, j].reshape(N_LAT_SEQ * PAST_LEN, KV_LORA), w_ukv2)
            kr_past = cache_mla_kpe[:, j].reshape(N_LAT_SEQ * PAST_LEN, QK_ROPE)
            att = _attention(qn, qr, kn_tok, v_tok, kpe, kn_past, v_past, kr_past)
            x = _res_linear(att, mla_w_out[j].astype(BF16), x, modt[l])
        else:
            x = _fourier(x, modt[l], n1, _dft_tables(), fn_w_out[j].astype(BF16))
        x = _moe(l, x, modt[l], norm2_g[l].reshape(1, D), router_w[l], router_b[l],
                 exp_w1, exp_b1, exp_w2, exp_b2, final_g.reshape(1, D), l == DEPTH - 1)

    y_prompt = x[:N_CTX].reshape(N_CTX_SEQ, CTX_T, D)
    y_sample = x[N_CTX:].reshape(N_LAT_SEQ, LAT_T, D)
    new_mla_ckv = new_ckv.reshape(N_CTX_SEQ, 1, CTX_T, KV_LORA)
    new_mla_kpe = new_kpe.reshape(N_CTX_SEQ, 1, CTX_T, QK_ROPE)
    st = jnp.stack([jnp.stack([sf, sb], axis=1) for sf, sb in states], axis=1)
    new_c = st[..., 0:ML_DV]
    new_n = st[..., ML_N_LANE]
    new_m = st[..., 0, ML_M_LANE]
    return (y_prompt, y_sample, new_mla_ckv, new_mla_kpe, new_c, new_n, new_m)
```

```python
import functools

import numpy as np
import jax
import jax.numpy as jnp
from jax import lax
from jax.experimental import pallas as pl
from jax.experimental.pallas import tpu as pltpu

F32 = jnp.float32
BF16 = jnp.bfloat16

D = 1024
DEPTH = 4
N_CTX_SEQ, CTX_T = 16, 256
N_LAT_SEQ, LAT_T = 2, 2048
N_CTX = N_CTX_SEQ * CTX_T
N_TOK = N_CTX + N_LAT_SEQ * LAT_T
TM = 256
N_TILES = N_TOK // TM
CTX_TILES = N_CTX // TM
LAT_TILES = LAT_T // TM
N_SEQ = N_CTX_SEQ + N_LAT_SEQ
PAST_LEN = 512
GRID_W = 64
RMS_EPS = 1e-6

ML_HEADS, ML_DK, ML_DV = 8, 64, 128
ML_AUG = 256
ML_N_LANE, ML_M_LANE = ML_DV, ML_DV + 1

MLA_HEADS, QK_NOPE, QK_ROPE, V_HEAD = 16, 128, 64, 128
Q_LORA, KV_LORA = 384, 256
ROPE_THETA = 10000.0
ATT_HG = 4
ATT_SCALE = (QK_NOPE + QK_ROPE) ** -0.5

FN_GROUPS, FN_GW = 4, 256

N_EXPERTS, TOP_K = 32, 4
SWIGLU_ALPHA, SWIGLU_LIMIT = 1.702, 7.0
MOE_BM = 256
N_ASSIGN = N_TOK * TOP_K
MOE_NB = N_ASSIGN // MOE_BM + N_EXPERTS
MOE_STEPS = MOE_NB + 1
MOE_SLOTS = 3
MOE_HC = 256
MOE_NHC = D // MOE_HC
MOE_RPC = MOE_BM // MOE_NHC
Y4_ROWS = N_ASSIGN + MOE_SLOTS * MOE_BM

VMEM_LIMIT = 56 * 1024 * 1024


def _cp(n_grid_axes, vmem=None, **kw):
    return pltpu.CompilerParams(dimension_semantics=("arbitrary",) * n_grid_axes, vmem_limit_bytes=vmem, **kw)


def _dot(a, b):
    return jnp.dot(a, b, preferred_element_type=F32)


def _dot_nt(a, b):
    return lax.dot_general(a, b, (((1,), (1,)), ((), ())), preferred_element_type=F32)


def _dot_tn(a, b):
    return lax.dot_general(a, b, (((0,), (0,)), ((), ())), preferred_element_type=F32)


def _sigmoid(x):
    return 1.0 / (1.0 + jnp.exp(-x))


def _log_sigmoid(x):
    return jnp.minimum(x, 0.0) - jnp.log(1.0 + jnp.exp(-jnp.abs(x)))


def _rms(x):
    return x * lax.rsqrt(jnp.mean(x * x, axis=-1, keepdims=True) + RMS_EPS)


def _norm_mod(x, g, shift, scale):
    return (_rms(x) * g) * (1.0 + scale) + shift


def _split2(x):
    hi = x.astype(BF16)
    lo = (x - hi.astype(F32)).astype(BF16)
    return hi, lo


def _split3(x):
    hi = x.astype(BF16)
    r = x - hi.astype(F32)
    mid = r.astype(BF16)
    lo = (r - mid.astype(F32)).astype(BF16)
    return hi, mid, lo


LANE_CHUNKS = D // 128


def _store_token_tiles(ref, row0, val):
    n = val.shape[0]
    for c in range(LANE_CHUNKS):
        ref[pl.ds(row0 * LANE_CHUNKS + c, n, stride=LANE_CHUNKS), :] = val[:, c * 128:(c + 1) * 128]


def _load_token_tile_chunk(ref, row0, n, c):
    return ref[pl.ds(row0 * LANE_CHUNKS + c, n, stride=LANE_CHUNKS), :]


PACK_CHUNKS = LANE_CHUNKS // 2
HI16 = 0xFFFF0000


def _store_packed_rows(ref, val):
    n = val.shape[0]
    for c in range(PACK_CHUNKS):
        lo = pltpu.bitcast(val[:, c * 128:(c + 1) * 128].astype(BF16).astype(F32), jnp.uint32)
        hi = pltpu.bitcast(val[:, D // 2 + c * 128:D // 2 + (c + 1) * 128].astype(BF16).astype(F32), jnp.uint32)
        ref[pl.ds(c, n, stride=PACK_CHUNKS), :] = (hi & jnp.uint32(HI16)) | (lo >> jnp.uint32(16))


def _load_packed_rows(ref, row0, n, out_ref):
    for c in range(PACK_CHUNKS):
        u = ref[pl.ds(row0 * PACK_CHUNKS + c, n, stride=PACK_CHUNKS), :]
        out_ref[:, c * 128:(c + 1) * 128] = pltpu.bitcast(u << jnp.uint32(16), F32).astype(BF16)
        out_ref[:, D // 2 + c * 128:D // 2 + (c + 1) * 128] = pltpu.bitcast(u & jnp.uint32(HI16), F32).astype(BF16)


def _tile_seq(r):
    return jnp.where(r < CTX_TILES, r, CTX_TILES + (r - CTX_TILES) // LAT_TILES)


def _lat_seq(r):
    return jnp.clip((r - CTX_TILES) // LAT_TILES, 0, N_LAT_SEQ - 1)


MOD_TN = 1536


def _mod_kernel(c_ref, w_ref, b_ref, o_ref):
    a = c_ref[...]
    s_hi, s_lo = _split2(a * _sigmoid(a))
    w_hi, w_lo = _split2(w_ref[0])
    o_ref[0] = _dot_split(s_hi, s_lo, w_hi, w_lo) + b_ref[0]


def _modulation(cond8, ada_w, ada_b):
    n_col = ada_w.shape[-1]
    return pl.pallas_call(
        _mod_kernel,
        out_shape=jax.ShapeDtypeStruct((DEPTH, 8, n_col), F32),
        grid=(DEPTH, n_col // MOD_TN),
        in_specs=[pl.BlockSpec((8, D), lambda l, j: (0, 0)),
                  pl.BlockSpec((1, D, MOD_TN), lambda l, j: (l, 0, j)),
                  pl.BlockSpec((1, 1, MOD_TN), lambda l, j: (l, 0, j))],
        out_specs=pl.BlockSpec((1, 8, MOD_TN), lambda l, j: (l, 0, j)),
        compiler_params=_cp(2, VMEM_LIMIT),
    )(cond8, ada_w, ada_b.reshape(DEPTH, 1, n_col))


def _ml_in_kernel(x_ref, mod_ref, g_ref, w_ref, wg_ref, wgt_ref, qkv_ref, o_ref, gg_ref, gt_ref):
    h = _norm_mod(x_ref[...], g_ref[...], mod_ref[0, 0:1, :], mod_ref[0, 1:2, :]).astype(BF16)
    hk = ML_HEADS * ML_DK
    lane = lax.broadcasted_iota(jnp.int32, (1, 2 * hk), 1)
    qscale = jnp.where(lane < hk, ML_DK ** -0.5, 1.0).astype(F32)
    qkv_ref[:, 0:2 * hk] = (_dot(h, w_ref[:, 0:2 * hk]) * qscale).astype(BF16)
    qkv_ref[:, 2 * hk:2 * hk + D] = _dot(h, w_ref[:, 2 * hk:2 * hk + D]).astype(BF16)
    o_ref[...] = _dot(h, w_ref[:, 2 * hk + D:2 * hk + 2 * D])
    gg_ref[...] = _dot(h, wg_ref[...])
    gt_ref[...] = _dot_nt(wgt_ref[...], h)


def _ml_in(x, modt, norm_g, w_main, w_g, w_gt):
    tile = lambda i: (i, 0)
    const = lambda i: (0, 0)
    return pl.pallas_call(
        _ml_in_kernel,
        out_shape=(jax.ShapeDtypeStruct((N_TOK, 2 * D), BF16),
                   jax.ShapeDtypeStruct((N_TOK, D), F32),
                   jax.ShapeDtypeStruct((N_TOK, 128), F32),
                   jax.ShapeDtypeStruct((32, N_TOK), F32)),
        grid=(N_TILES,),
        in_specs=[pl.BlockSpec((TM, D), tile),
                  pl.BlockSpec((1, 6, D), lambda i: (i, 0, 0)),
                  pl.BlockSpec((1, D), const),
                  pl.BlockSpec((D, 3 * D), const),
                  pl.BlockSpec((D, 128), const),
                  pl.BlockSpec((32, D), const)],
        out_specs=(pl.BlockSpec((TM, 2 * D), tile),
                   pl.BlockSpec((TM, D), tile),
                   pl.BlockSpec((TM, 128), tile),
                   pl.BlockSpec((32, TM), lambda i: (0, i))),
        compiler_params=_cp(1, VMEM_LIMIT),
    )(x, modt, norm_g, w_main, w_g, w_gt)


def _ml_direction(lower, q_ref, k_ref, v_ref, g_ref, gt_ref, gbrow_ref, gbcol_ref, h_ref, st_ref, c_s, m_s):
    L = TM
    d_off = 0 if lower else ML_HEADS
    row = lax.broadcasted_iota(jnp.int32, (L, L), 0)
    col = lax.broadcasted_iota(jnp.int32, (L, L), 1)
    mask = (col <= row) if lower else (col >= row)
    tri = jnp.where(mask, 1.0, 0.0).astype(BF16)

    lane = lax.broadcasted_iota(jnp.int32, (L, 128), 1)
    gc = g_ref[...] + gbrow_ref[...]
    gc = jnp.where((lane >= 2 * ML_HEADS) & (lane < 4 * ML_HEADS), _log_sigmoid(gc), gc)
    bc = sum(_dot(tri, p) for p in _split3(gc))
    sub = lax.broadcasted_iota(jnp.int32, (32, L), 0)
    gr = gt_ref[...] + gbcol_ref[:, 0:1]
    gr = jnp.where(sub >= 2 * ML_HEADS, _log_sigmoid(gr), gr)
    br = sum(_dot_nt(p, tri) for p in _split3(gr))

    q = q_ref[...]
    k = k_ref[...]
    v = v_ref[...]
    lane_a = lax.broadcasted_iota(jnp.int32, (L, ML_AUG - ML_DV), 1)
    ones_blk = jnp.where(lane_a == 0, 1.0, 0.0).astype(BF16)
    lane_s = lax.broadcasted_iota(jnp.int32, (ML_DK, ML_AUG), 1)

    for hd in range(ML_HEADS):
        ci = d_off + hd
        cf = 2 * ML_HEADS + d_off + hd
        i_col, b_col = gc[:, ci:ci + 1], bc[:, cf:cf + 1]
        i_row, b_row = gr[ci:ci + 1, :], br[cf:cf + 1, :]
        total = b_col[L - 1:L, :] if lower else b_col[0:1, :]
        m = m_s[hd][0:1, 0:1]
        c_aug = c_s[hd]
        dmat = jnp.where(mask, (b_col - b_row) + i_row, -jnp.inf)
        mi = jnp.max(dmat, axis=1, keepdims=True)
        a_col = b_col + m
        m_row = jnp.maximum(a_col, mi)
        qh = q[:, hd * ML_DK:(hd + 1) * ML_DK]
        kh = k[:, hd * ML_DK:(hd + 1) * ML_DK]
        v_aug = jnp.concatenate([v[:, hd * ML_DV:(hd + 1) * ML_DV], ones_blk], axis=1)
        p = (_dot_nt(qh, kh) * jnp.exp(dmat - mi)).astype(BF16)
        num = jnp.exp(mi - m_row) * _dot(p, v_aug) + jnp.exp(a_col - m_row) * _dot(qh, c_aug.astype(BF16))
        den = jnp.maximum(jnp.abs(num[:, ML_N_LANE:ML_N_LANE + 1]), jnp.exp(-m_row))
        h_ref[:, hd * ML_DV:(hd + 1) * ML_DV] = num[:, 0:ML_DV] / den

        g_col = (total - b_col) + i_col
        m_new = jnp.maximum(total + m, jnp.max(g_col, axis=0, keepdims=True))
        kw = (kh.astype(F32) * jnp.exp(g_col - m_new)).astype(BF16)
        c_new = jnp.exp(total + m - m_new) * c_aug + _dot_tn(kw, v_aug)
        c_s[hd] = c_new
        m_s[hd] = jnp.broadcast_to(m_new, (8, 128))
        st_ref[0, hd] = jnp.where(lane_s == ML_M_LANE, m_new, c_new)


def _ml_core_kernel(qf, kf, vf, gf, gtf, qb, kb, vb, gb, gtb, gbrow, gbcol, initf, initb,
                    hf_ref, hb_ref, stf_ref, stb_ref, cf_s, cb_s, mf_s, mb_s):
    i = pl.program_id(0)
    start_f = (i <= CTX_TILES) | (i == CTX_TILES + LAT_TILES)
    start_b = (i == 0) | (i == LAT_TILES) | (i >= 2 * LAT_TILES)

    def load_state(init_ref, c_s, m_s):
        c_s[...] = init_ref[0]
        for hd in range(ML_HEADS):
            m_s[hd] = jnp.broadcast_to(init_ref[0, hd][0:1, ML_M_LANE:ML_M_LANE + 1], (8, 128))

    @pl.when(start_f)
    def _():
        load_state(initf, cf_s, mf_s)

    @pl.when(start_b)
    def _():
        load_state(initb, cb_s, mb_s)

    _ml_direction(True, qf, kf, vf, gf, gtf, gbrow, gbcol, hf_ref, stf_ref, cf_s, mf_s)
    _ml_direction(False, qb, kb, vb, gb, gtb, gbrow, gbcol, hb_ref, stb_ref, cb_s, mb_s)


def _ml_core(qkv, gg, gt, gb_row, gb_col, init_f, init_b):
    last = N_TILES - 1
    fwd = lambda i: i
    bwd = lambda i: last - i
    hk = ML_HEADS * ML_DK

    def specs(t):
        return [pl.BlockSpec((TM, hk), lambda i: (t(i), 0)),
                pl.BlockSpec((TM, hk), lambda i: (t(i), 1)),
                pl.BlockSpec((TM, D), lambda i: (t(i), 1)),
                pl.BlockSpec((TM, 128), lambda i: (t(i), 0)),
                pl.BlockSpec((32, TM), lambda i: (0, t(i)))]

    def st_spec(t):
        return pl.BlockSpec((1, ML_HEADS, ML_DK, ML_AUG), lambda i: (_tile_seq(t(i)), 0, 0, 0))

    st_shape = jax.ShapeDtypeStruct((N_SEQ, ML_HEADS, ML_DK, ML_AUG), F32)
    return pl.pallas_call(
        _ml_core_kernel,
        out_shape=(jax.ShapeDtypeStruct((N_TOK, D), F32), jax.ShapeDtypeStruct((N_TOK, D), F32),
                   st_shape, st_shape),
        grid=(N_TILES,),
        in_specs=specs(fwd) + specs(bwd) + [
            pl.BlockSpec((1, 128), lambda i: (0, 0)),
            pl.BlockSpec((32, 128), lambda i: (0, 0)),
            st_spec(fwd), st_spec(bwd)],
        out_specs=(pl.BlockSpec((TM, D), lambda i: (i, 0)),
                   pl.BlockSpec((TM, D), lambda i: (last - i, 0)),
                   st_spec(fwd), st_spec(bwd)),
        scratch_shapes=[pltpu.VMEM((ML_HEADS, ML_DK, ML_AUG), F32),
                        pltpu.VMEM((ML_HEADS, ML_DK, ML_AUG), F32),
                        pltpu.VMEM((ML_HEADS, 8, 128), F32),
                        pltpu.VMEM((ML_HEADS, 8, 128), F32)],
        compiler_params=_cp(1, VMEM_LIMIT),
    )(qkv, qkv, qkv, gg, gt, qkv, qkv, qkv, gg, gt, gb_row, gb_col, init_f, init_b)


def _ml_out_kernel(hf_ref, hb_ref, o_ref, ng_ref, w_ref, x_ref, mod_ref, out_ref, z_s):
    hh = hf_ref[...] + hb_ref[...]
    o = o_ref[...]
    for hd in range(ML_HEADS):
        sl = slice(hd * ML_DV, (hd + 1) * ML_DV)
        z_s[:, sl] = (_sigmoid(o[:, sl]) * (_rms(hh[:, sl]) * ng_ref[:, sl])).astype(BF16)
    out_ref[...] = x_ref[...] + mod_ref[0, 2:3, :] * _dot(z_s[...], w_ref[...])


def _ml_out(hf, hb, o, norm_g, w_out, x, modt):
    tile = lambda i: (i, 0)
    const = lambda i: (0, 0)
    return pl.pallas_call(
        _ml_out_kernel,
        out_shape=jax.ShapeDtypeStruct((N_TOK, D), F32),
        grid=(N_TILES,),
        in_specs=[pl.BlockSpec((TM, D), tile), pl.BlockSpec((TM, D), tile), pl.BlockSpec((TM, D), tile),
                  pl.BlockSpec((1, D), const), pl.BlockSpec((D, D), const),
                  pl.BlockSpec((TM, D), tile), pl.BlockSpec((1, 6, D), lambda i: (i, 0, 0))],
        out_specs=pl.BlockSpec((TM, D), tile),
        scratch_shapes=[pltpu.VMEM((TM, D), BF16)],
        compiler_params=_cp(1, VMEM_LIMIT),
    )(hf, hb, o, norm_g, w_out, x, modt)


MLA_IN_COLS = 896
NOPE_ALL = MLA_HEADS * QK_NOPE
ROPE_ALL = MLA_HEADS * QK_ROPE


def _mla_in_kernel(x_ref, mod_ref, g_ref, w_ref, qg_ref, kvg_ref, wuq_ref, cos_ref, sin_ref,
                   qn_ref, qr_ref, ckv_ref, kpe_ref):
    h = _norm_mod(x_ref[...], g_ref[...], mod_ref[0, 0:1, :], mod_ref[0, 1:2, :]).astype(BF16)
    c = _dot(h, w_ref[...])
    cos, sin = cos_ref[...], sin_ref[...]
    ckv_ref[...] = _rms(c[:, Q_LORA:Q_LORA + KV_LORA]) * kvg_ref[...]
    kpe_ref[...] = c[:, 640:704] * cos[:, 0:QK_ROPE] + c[:, 768:832] * sin[:, 0:QK_ROPE]
    cq = (_rms(c[:, 0:Q_LORA]) * qg_ref[...]).astype(BF16)
    qn_ref[...] = _dot(cq, wuq_ref[:, 0:NOPE_ALL]).astype(BF16)
    rope = _dot(cq, wuq_ref[:, NOPE_ALL:NOPE_ALL + ROPE_ALL])
    swapped = _dot(cq, wuq_ref[:, NOPE_ALL + ROPE_ALL:NOPE_ALL + 2 * ROPE_ALL])
    for s in range(ROPE_ALL // 128):
        sl = slice(s * 128, (s + 1) * 128)
        qr_ref[:, sl] = (rope[:, sl] * cos + swapped[:, sl] * sin).astype(BF16)


def _mla_in(x, modt, norm_g, w_in2, qg, kvg, w_uq2, cos2, sin2):
    tile = lambda i: (i, 0)
    const = lambda i: (0, 0)
    return pl.pallas_call(
        _mla_in_kernel,
        out_shape=(jax.ShapeDtypeStruct((N_TOK, NOPE_ALL), BF16),
                   jax.ShapeDtypeStruct((N_TOK, ROPE_ALL), BF16),
                   jax.ShapeDtypeStruct((N_TOK, KV_LORA), F32),
                   jax.ShapeDtypeStruct((N_TOK, QK_ROPE), F32)),
        grid=(N_TILES,),
        in_specs=[pl.BlockSpec((TM, D), tile), pl.BlockSpec((1, 6, D), lambda i: (i, 0, 0)),
                  pl.BlockSpec((1, D), const), pl.BlockSpec((D, MLA_IN_COLS), const),
                  pl.BlockSpec((1, Q_LORA), const), pl.BlockSpec((1, KV_LORA), const),
                  pl.BlockSpec((Q_LORA, NOPE_ALL + 2 * ROPE_ALL), const),
                  pl.BlockSpec((TM, 128), tile), pl.BlockSpec((TM, 128), tile)],
        out_specs=(pl.BlockSpec((TM, NOPE_ALL), tile), pl.BlockSpec((TM, ROPE_ALL), tile),
                   pl.BlockSpec((TM, KV_LORA), tile), pl.BlockSpec((TM, QK_ROPE), tile)),
        compiler_params=_cp(1, VMEM_LIMIT),
    )(x, modt, norm_g, w_in2, qg, kvg, w_uq2, cos2, sin2)


def _kv_expand_kernel(c_ref, w_ref, kn_ref, v_ref):
    c = c_ref[...].astype(BF16)
    kn_ref[...] = _dot(c, w_ref[:, 0:NOPE_ALL]).astype(BF16)
    v_ref[...] = _dot(c, w_ref[:, NOPE_ALL:2 * NOPE_ALL]).astype(BF16)


def _kv_expand(ckv, w_ukv2):
    rows = ckv.shape[0]
    tile = lambda i: (i, 0)
    shp = jax.ShapeDtypeStruct((rows, NOPE_ALL), BF16)
    return pl.pallas_call(
        _kv_expand_kernel,
        out_shape=(shp, shp),
        grid=(rows // TM,),
        in_specs=[pl.BlockSpec((TM, KV_LORA), tile), pl.BlockSpec((KV_LORA, 2 * NOPE_ALL), lambda i: (0, 0))],
        out_specs=(pl.BlockSpec((TM, NOPE_ALL), tile), pl.BlockSpec((TM, NOPE_ALL), tile)),
        compiler_params=_cp(1, VMEM_LIMIT),
    )(ckv, w_ukv2)


def _attend(qn_ref, qr_ref, segs, out_ref):
    kr = [s[2][...].astype(BF16) for s in segs]
    for hd in range(ATT_HG):
        sl = slice(hd * QK_NOPE, (hd + 1) * QK_NOPE)
        qn = qn_ref[:, sl]
        qr = qr_ref[:, hd * QK_ROPE:(hd + 1) * QK_ROPE]
        scores = [(_dot_nt(qn, s[0][:, sl]) + _dot_nt(qr, kr_s)) * ATT_SCALE for s, kr_s in zip(segs, kr)]
        m = functools.reduce(jnp.maximum, [jnp.max(s, axis=1, keepdims=True) for s in scores])
        ps = [jnp.exp(s - m) for s in scores]
        denom = sum(jnp.sum(p, axis=1, keepdims=True) for p in ps)
        acc = sum(_dot(p.astype(BF16), s[1][:, sl]) for p, s in zip(ps, segs))
        out_ref[:, sl] = (acc / denom).astype(BF16)


def _attn_kernel(qn_ref, qr_ref, knc_ref, vc_ref, krc_ref, knl_ref, vl_ref, krl_ref, knp_ref, vp_ref, krp_ref,
                 out_ref):
    r = pl.program_id(1)

    @pl.when(r < CTX_TILES)
    def _():
        _attend(qn_ref, qr_ref, [(knc_ref, vc_ref, krc_ref)], out_ref)

    @pl.when(r >= CTX_TILES)
    def _():
        _attend(qn_ref, qr_ref, [(knp_ref, vp_ref, krp_ref), (knl_ref, vl_ref, krl_ref)], out_ref)


def _attention(qn, qr, kn_tok, v_tok, kr_tok, kn_past, v_past, kr_past):
    hw = ATT_HG * QK_NOPE
    ctx_t = lambda g, r: (jnp.minimum(r, CTX_TILES - 1), g)
    lat_t = lambda g, r: (N_CTX // LAT_T + _lat_seq(r), g)
    past_t = lambda g, r: (_lat_seq(r), g)
    col0 = lambda f: (lambda g, r: (f(g, r)[0], 0))
    return pl.pallas_call(
        _attn_kernel,
        out_shape=jax.ShapeDtypeStruct((N_TOK, NOPE_ALL), BF16),
        grid=(MLA_HEADS // ATT_HG, N_TILES),
        in_specs=[pl.BlockSpec((TM, hw), lambda g, r: (r, g)),
                  pl.BlockSpec((TM, ATT_HG * QK_ROPE), lambda g, r: (r, g)),
                  pl.BlockSpec((CTX_T, hw), ctx_t), pl.BlockSpec((CTX_T, hw), ctx_t),
                  pl.BlockSpec((CTX_T, QK_ROPE), col0(ctx_t)),
                  pl.BlockSpec((LAT_T, hw), lat_t), pl.BlockSpec((LAT_T, hw), lat_t),
                  pl.BlockSpec((LAT_T, QK_ROPE), col0(lat_t)),
                  pl.BlockSpec((PAST_LEN, hw), past_t), pl.BlockSpec((PAST_LEN, hw), past_t),
                  pl.BlockSpec((PAST_LEN, QK_ROPE), col0(past_t))],
        out_specs=pl.BlockSpec((TM, hw), lambda g, r: (r, g)),
        compiler_params=_cp(2, VMEM_LIMIT),
    )(qn, qr, kn_tok, v_tok, kr_tok, kn_tok, v_tok, kr_tok, kn_past, v_past, kr_past)


def _res_linear_kernel(a_ref, w_ref, x_ref, mod_ref, out_ref):
    out_ref[...] = x_ref[...] + mod_ref[0, 2:3, :] * _dot(a_ref[...], w_ref[...])


def _res_linear(a, w, x, modt):
    k = a.shape[1]
    tile = lambda i: (i, 0)
    return pl.pallas_call(
        _res_linear_kernel,
        out_shape=jax.ShapeDtypeStruct((N_TOK, D), F32),
        grid=(N_TILES,),
        in_specs=[pl.BlockSpec((TM, k), tile), pl.BlockSpec((k, D), lambda i: (0, 0)),
                  pl.BlockSpec((TM, D), tile), pl.BlockSpec((1, 6, D), lambda i: (i, 0, 0))],
        out_specs=pl.BlockSpec((TM, D), tile),
        compiler_params=_cp(1, VMEM_LIMIT),
    )(a, w, x, modt)


def _dot_split(a_hi, a_lo, b_hi, b_lo):
    return _dot(a_hi, b_hi) + (_dot(a_hi, b_lo) + _dot(a_lo, b_hi))


def _fn_channel_dft(x, g, shift, scale, wc_hi, wc_lo):
    h_hi, h_lo = _split2(_norm_mod(x, g, shift, scale))
    a_parts, b_parts = [], []
    for grp in range(FN_GROUPS):
        sl = slice(grp * FN_GW, (grp + 1) * FN_GW)
        ab = _dot_split(h_hi[:, sl], h_lo[:, sl], wc_hi, wc_lo)
        a_parts.append(ab[:, 0:FN_GW])
        b_parts.append(ab[:, FN_GW:2 * FN_GW])
    return jnp.concatenate(a_parts, axis=1), jnp.concatenate(b_parts, axis=1)


FN_STEPS = CTX_TILES + N_LAT_SEQ * 2 * LAT_TILES


def _fn_step(t):
    u = jnp.maximum(t - CTX_TILES, 0)
    seq, ph, tile = u // (2 * LAT_TILES), (u % (2 * LAT_TILES)) // LAT_TILES, u % LAT_TILES
    is_ctx = t < CTX_TILES
    return is_ctx, ph, tile, jnp.where(is_ctx, t, CTX_TILES + seq * LAT_TILES + tile)


def _fn_kernel(x_ref, mod_ref, g_ref, wc_ref, tc_ref, tl_ref, w_ref, out_ref, ab_hi_s, ab_lo_s):
    is_ctx, ph, tile, _ = _fn_step(pl.program_id(0))
    g = g_ref[...]
    shift, scale, gate = mod_ref[0, 0:1, :], mod_ref[0, 1:2, :], mod_ref[0, 2:3, :]
    wc_hi, wc_lo = _split2(wc_ref[...])

    def finish(f, t_len):
        f = f * ((t_len * FN_GW) ** -0.5)
        out_ref[...] = x_ref[...] + gate * _dot(f.astype(BF16), w_ref[...])

    @pl.when(is_ctx)
    def _():
        a, b = _fn_channel_dft(x_ref[...], g, shift, scale, wc_hi, wc_lo)
        ab_hi, ab_lo = _split2(jnp.concatenate([a, b], axis=0))
        finish(_dot_split(*_split2(tc_ref[...]), ab_hi, ab_lo), CTX_T)

    @pl.when(jnp.logical_not(is_ctx) & (ph == 0))
    def _():
        a, b = _fn_channel_dft(x_ref[...], g, shift, scale, wc_hi, wc_lo)
        a_hi, a_lo = _split2(a)
        b_hi, b_lo = _split2(b)
        rows = pl.ds(pl.multiple_of(tile * TM, TM), TM)
        rows_b = pl.ds(pl.multiple_of(LAT_T + tile * TM, TM), TM)
        ab_hi_s[rows, :] = a_hi
        ab_lo_s[rows, :] = a_lo
        ab_hi_s[rows_b, :] = b_hi
        ab_lo_s[rows_b, :] = b_lo

    @pl.when(jnp.logical_not(is_ctx) & (ph == 1))
    def _():
        finish(_dot_split(*_split2(tl_ref[...]), ab_hi_s[...], ab_lo_s[...]), LAT_T)


def _fourier(x, modt, norm_g, tabs, w_out):
    wc, tc, tl = tabs
    const = lambda t: (0, 0)
    x_tile = lambda t: (_fn_step(t)[3], 0)

    def out_tile(t):
        is_ctx, ph, tile, gt = _fn_step(t)
        return (jnp.where(is_ctx | (ph == 1), gt, gt - tile), 0)

    def tab_row(t):
        is_ctx, ph, tile, _ = _fn_step(t)
        return (jnp.where(is_ctx | (ph == 0), 0, tile), 0)

    return pl.pallas_call(
        _fn_kernel,
        out_shape=jax.ShapeDtypeStruct((N_TOK, D), F32),
        grid=(FN_STEPS,),
        in_specs=[pl.BlockSpec((TM, D), x_tile),
                  pl.BlockSpec((1, 6, D), lambda t: (_fn_step(t)[3], 0, 0)),
                  pl.BlockSpec((1, D), const),
                  pl.BlockSpec((FN_GW, 2 * FN_GW), const),
                  pl.BlockSpec((CTX_T, 2 * CTX_T), const),
                  pl.BlockSpec((TM, 2 * LAT_T), tab_row),
                  pl.BlockSpec((D, D), const)],
        out_specs=pl.BlockSpec((TM, D), out_tile),
        scratch_shapes=[pltpu.VMEM((2 * LAT_T, D), BF16), pltpu.VMEM((2 * LAT_T, D), BF16)],
        compiler_params=_cp(1, VMEM_LIMIT),
    )(x, modt, norm_g, wc, tc, tl, w_out)


@functools.lru_cache(maxsize=None)
def _dft_tables():
    def cos_sin(n):
        k = np.arange(n, dtype=np.int64)
        ang = ((k[:, None] * k[None, :]) % n).astype(np.float64) * (2.0 * np.pi / n)
        return np.cos(ang), np.sin(ang)

    cc, sc = cos_sin(FN_GW)
    out = [np.concatenate([cc, sc], axis=1).astype(np.float32)]
    for t_len in (CTX_T, LAT_T):
        ct, st = cos_sin(t_len)
        out.append(np.concatenate([ct, -st], axis=1).astype(np.float32))
    return tuple(out)


def _router_kernel(x_ref, mod_ref, g_ref, rw_hi_ref, rw_lo_ref, rb_ref, h_ref, idt_ref, gate_ref, cnt_ref):
    h = _norm_mod(x_ref[...], g_ref[...], mod_ref[0, 3:4, :], mod_ref[0, 4:5, :])
    _store_packed_rows(h_ref, h)
    h_hi, h_lo = _split2(h)
    logits = _dot_split(h_hi, h_lo, rw_hi_ref[...], rw_lo_ref[...]) + rb_ref[...]
    lane = lax.broadcasted_iota(jnp.int32, logits.shape, 1)
    lane_f = lane.astype(F32)
    cur = jnp.where(lane < N_EXPERTS, logits, -jnp.inf)
    vals, idxs = [], []
    for _ in range(TOP_K):
        m = jnp.max(cur, axis=1, keepdims=True)
        ik = jnp.min(jnp.where(cur == m, lane_f, 128.0), axis=1, keepdims=True).astype(jnp.int32)
        vals.append(m)
        idxs.append(ik)
        cur = jnp.where(lane == ik, -jnp.inf, cur)
    es = [jnp.exp(v - vals[0]) for v in vals]
    denom = functools.reduce(lambda a, b: a + b, es)
    idx_out = jnp.zeros(logits.shape, F32)
    gate_out = jnp.zeros(logits.shape, F32)
    for kk in range(TOP_K):
        idx_out = jnp.where(lane == kk, idxs[kk].astype(F32), idx_out)
        gate_out = jnp.where(lane == kk, es[kk] / denom, gate_out)
    idt_ref[...] = jnp.transpose(idx_out)[0:8, :].astype(jnp.int32)
    gate_ref[...] = gate_out

    @pl.when(pl.program_id(0) == 0)
    def _():
        cnt_ref[...] = jnp.zeros(cnt_ref.shape, F32)

    hits = functools.reduce(lambda a, b: a + b, [jnp.where(lane == ik, 1.0, 0.0) for ik in idxs])
    cnt_ref[...] += jnp.sum(hits, axis=0, keepdims=True)


def _router(x, modt, norm_g, rw_hi, rw_lo, rb):
    tile = lambda i: (i, 0)
    const = lambda i: (0, 0)
    return pl.pallas_call(
        _router_kernel,
        out_shape=(jax.ShapeDtypeStruct((N_TOK * PACK_CHUNKS, 128), jnp.uint32),
                   jax.ShapeDtypeStruct((8, N_TOK), jnp.int32),
                   jax.ShapeDtypeStruct((N_TOK, 128), F32),
                   jax.ShapeDtypeStruct((1, 128), F32)),
        grid=(N_TILES,),
        in_specs=[pl.BlockSpec((TM, D), tile), pl.BlockSpec((1, 6, D), lambda i: (i, 0, 0)),
                  pl.BlockSpec((1, D), const), pl.BlockSpec((D, 128), const), pl.BlockSpec((D, 128), const),
                  pl.BlockSpec((1, 128), const)],
        out_specs=(pl.BlockSpec((TM * PACK_CHUNKS, 128), tile), pl.BlockSpec((8, TM), lambda i: (0, i)),
                   pl.BlockSpec((TM, 128), tile), pl.BlockSpec((1, 128), const)),
        compiler_params=_cp(1, VMEM_LIMIT),
    )(x, modt, norm_g, rw_hi, rw_lo, rb)


def _route_plan(idt, cnt):
    counts = cnt[0, 0:N_EXPERTS].astype(jnp.int32)
    nblk = (counts + MOE_BM - 1) // MOE_BM
    total = jnp.sum(nblk)
    pad = nblk * MOE_BM - counts
    experts = jnp.arange(N_EXPERTS, dtype=jnp.int32)
    filler = jnp.where(jnp.arange(MOE_BM - 1, dtype=jnp.int32)[None, :] < pad[:, None], experts[:, None], N_EXPERTS)
    n_tail = MOE_STEPS * MOE_BM - N_ASSIGN - N_EXPERTS * (MOE_BM - 1)
    keys = jnp.concatenate([idt[0:TOP_K].reshape(-1), filler.reshape(-1), jnp.full((n_tail,), N_EXPERTS, jnp.int32)])
    skeys, order = lax.sort((keys, jnp.arange(keys.shape[0], dtype=jnp.int32)), num_keys=1, is_stable=True)
    b = jnp.arange(MOE_STEPS, dtype=jnp.int32)
    be = skeys[::MOE_BM]
    be = jnp.where(b < total, be, be[jnp.maximum(total - 1, 0)])
    first = (((b == 0) | (be != jnp.concatenate([be[:1], be[:-1]]))) & (b < total)).astype(jnp.int32)
    return be, first, total.reshape(1), order


def _expert_kernel(be_ref, first_ref, total_ref, order_ref,
                   x_hbm, w1_ref, b1_ref, w2_ref, b2_ref, y4_hbm,
                   xs_buf, y_buf, y_acc, xs_bf, w1b, w2b, gsem, ssem):
    b = pl.program_id(0)
    total = total_ref[0]

    def tile_rows(r, per_row=LANE_CHUNKS):
        return pl.ds(pl.multiple_of(r * per_row, per_row), per_row)

    def slot_rows(slot, per_row=LANE_CHUNKS):
        return pl.ds(pl.multiple_of(slot * (MOE_BM * per_row), MOE_BM * per_row), MOE_BM * per_row)

    def gather_start(pos0, slot, row0, i):
        a = order_ref[pos0 + row0 + i]
        pltpu.make_async_copy(x_hbm.at[tile_rows(a & (N_TOK - 1), PACK_CHUNKS), :],
                              xs_buf.at[tile_rows(slot * MOE_BM + row0 + i, PACK_CHUNKS), :], gsem.at[slot]).start()

    def scatter_start(pos0, slot, row0, i):
        row = row0 + i
        a = order_ref[pos0 + row]
        dst = jnp.where(a < N_ASSIGN, a, N_ASSIGN + slot * MOE_BM + row)
        pltpu.make_async_copy(y_buf.at[tile_rows(slot * MOE_BM + row), :], y4_hbm.at[tile_rows(dst), :],
                              ssem.at[slot]).start(priority=1)

    def wait_gather(slot):
        pltpu.make_async_copy(x_hbm.at[slot_rows(0, PACK_CHUNKS), :], xs_buf.at[slot_rows(slot, PACK_CHUNKS), :],
                              gsem.at[slot]).wait()

    def wait_scatter(slot):
        pltpu.make_async_copy(y_buf.at[slot_rows(slot), :], y4_hbm.at[slot_rows(0), :], ssem.at[slot]).wait()

    def row_loop(fn):
        def body(g, c):
            for i in range(8):
                fn(g * 8, i)
            return c
        lax.fori_loop(0, MOE_BM // 8, body, 0)

    @pl.when(b == 0)
    def _():
        y_buf[...] = jnp.zeros(y_buf.shape, F32)
        for s in range(2):
            pltpu.make_async_copy(y_buf.at[slot_rows(s), :], y4_hbm.at[slot_rows(N_ASSIGN // MOE_BM + s), :],
                                  ssem.at[s]).start()
        row_loop(functools.partial(gather_start, 0, 0))
        row_loop(functools.partial(gather_start, jnp.minimum(1, total - 1) * MOE_BM, 1))

    @pl.when(b < total)
    def _():
        slot = b % MOE_SLOTS

        @pl.when(first_ref[b] == 1)
        def _():
            for j in range(2 * MOE_NHC):
                w1b[j] = w1_ref[0, 0, :, j * MOE_HC:(j + 1) * MOE_HC].astype(BF16)
            for j in range(MOE_NHC):
                w2b[j] = w2_ref[0, 0, j * MOE_HC:(j + 1) * MOE_HC, :].astype(BF16)

        wait_gather(slot)
        wait_scatter(slot)
        _load_packed_rows(xs_buf, slot * MOE_BM, MOE_BM, xs_bf)
        y_acc[...] = jnp.broadcast_to(b2_ref[0, 0], (MOE_BM, D))

        npos = jnp.minimum(b + 2, total - 1) * MOE_BM
        ppos = jnp.where(b >= 1, b - 1, MOE_STEPS - 1) * MOE_BM
        oslot = (b + 2) % MOE_SLOTS

        def hidden_chunk(j, c):
            xb = xs_bf[...]
            hg = _dot(xb, w1b[j]) + b1_ref[0, 0, pl.ds(j, 1), :]
            hu = _dot(xb, w1b[MOE_NHC + j]) + b1_ref[0, 0, pl.ds(MOE_NHC + j, 1), :]
            gate = jnp.minimum(hg, SWIGLU_LIMIT)
            up = jnp.clip(hu, -SWIGLU_LIMIT, SWIGLU_LIMIT)
            act = ((up + 1.0) * (gate * _sigmoid(SWIGLU_ALPHA * gate))).astype(BF16)
            y_acc[...] += _dot(act, w2b[j])
            for i in range(MOE_RPC):
                gather_start(npos, oslot, j * MOE_RPC, i)
                scatter_start(ppos, oslot, j * MOE_RPC, i)
            return c
        lax.fori_loop(0, MOE_NHC, hidden_chunk, 0)
        _store_token_tiles(y_buf, slot * MOE_BM, y_acc[...])

    @pl.when(b == total)
    def _():
        last = total - 1
        row_loop(functools.partial(scatter_start, last * MOE_BM, last % MOE_SLOTS))

    @pl.when(b == MOE_STEPS - 1)
    def _():
        for s in range(MOE_SLOTS):
            wait_scatter(s)
        wait_gather(total % MOE_SLOTS)
        wait_gather((total + 1) % MOE_SLOTS)


def _experts(layer, plan, h2, w1, b1, w2, b2):
    be, first, total, order = plan
    wmap = lambda b, be, *_: (layer, be[b], 0, 0)
    grid_spec = pltpu.PrefetchScalarGridSpec(
        num_scalar_prefetch=4,
        grid=(MOE_STEPS,),
        in_specs=[pl.BlockSpec(memory_space=pl.ANY),
                  pl.BlockSpec((1, 1, D, 2 * D), wmap), pl.BlockSpec((1, 1, 2 * MOE_NHC, MOE_HC), wmap),
                  pl.BlockSpec((1, 1, D, D), wmap), pl.BlockSpec((1, 1, 1, D), wmap)],
        out_specs=pl.BlockSpec(memory_space=pl.ANY),
        scratch_shapes=[pltpu.VMEM((MOE_SLOTS * MOE_BM * PACK_CHUNKS, 128), jnp.uint32),
                        pltpu.VMEM((MOE_SLOTS * MOE_BM * LANE_CHUNKS, 128), F32),
                        pltpu.VMEM((MOE_BM, D), F32), pltpu.VMEM((MOE_BM, D), BF16),
                        pltpu.VMEM((2 * MOE_NHC, D, MOE_HC), BF16), pltpu.VMEM((MOE_NHC, MOE_HC, D), BF16),
                        pltpu.SemaphoreType.DMA((MOE_SLOTS,)), pltpu.SemaphoreType.DMA((MOE_SLOTS,))],
    )
    return pl.pallas_call(
        _expert_kernel,
        out_shape=jax.ShapeDtypeStruct((Y4_ROWS * LANE_CHUNKS, 128), F32),
        grid_spec=grid_spec,
        compiler_params=_cp(1, VMEM_LIMIT, disable_bounds_checks=True),
    )(be, first, total, order, h2, w1, b1.reshape(DEPTH, N_EXPERTS, 2 * MOE_NHC, MOE_HC), w2,
      b2.reshape(DEPTH, N_EXPERTS, 1, D))


def _combine_kernel(final, x_ref, y0, y1, y2, y3, gate_ref, mod_ref, fg_ref, out_ref):
    gates = gate_ref[...]
    parts = []
    for c in range(LANE_CHUNKS):
        moe = gates[:, 0:1] * _load_token_tile_chunk(y0, 0, TM, c)
        for kk, y in ((1, y1), (2, y2), (3, y3)):
            moe = moe + gates[:, kk:kk + 1] * _load_token_tile_chunk(y, 0, TM, c)
        parts.append(moe)
    x = x_ref[...] + mod_ref[0, 5:6, :] * jnp.concatenate(parts, axis=1)
    out_ref[...] = _rms(x) * fg_ref[...] if final else x


def _combine(x, y4, gate, modt, final_g, final):
    tile = lambda i: (i, 0)
    plane = lambda kk: pl.BlockSpec((TM * LANE_CHUNKS, 128), lambda i: (kk * N_TILES + i, 0))
    return pl.pallas_call(
        functools.partial(_combine_kernel, final),
        out_shape=jax.ShapeDtypeStruct((N_TOK, D), F32),
        grid=(N_TILES,),
        in_specs=[pl.BlockSpec((TM, D), tile), plane(0), plane(1), plane(2), plane(3),
                  pl.BlockSpec((TM, 128), tile), pl.BlockSpec((1, 6, D), lambda i: (i, 0, 0)),
                  pl.BlockSpec((1, D), lambda i: (0, 0))],
        out_specs=pl.BlockSpec((TM, D), tile),
        compiler_params=_cp(1, VMEM_LIMIT),
    )(x, y4, y4, y4, y4, gate, modt, final_g)


def _moe(layer, x, modt, norm2_g, router_w, router_b, w1, b1, w2, b2, final_g, final):
    rw = jnp.pad(router_w, ((0, 0), (0, 128 - N_EXPERTS)))
    rw_hi, rw_lo = _split2(rw)
    rb = jnp.pad(router_b, (0, 128 - N_EXPERTS)).reshape(1, 128)
    h2, idt, gate, cnt = _router(x, modt, norm2_g, rw_hi, rw_lo, rb)
    y4 = _experts(layer, _route_plan(idt, cnt), h2, w1, b1, w2, b2)
    return _combine(x, y4, gate, modt, final_g, final)


def _rope_tables():
    def tables(n_tokens):
        rows = n_tokens // GRID_W
        row = jnp.repeat(jnp.arange(rows, dtype=F32), GRID_W)
        col = jnp.tile(jnp.arange(GRID_W, dtype=F32), rows)
        n_freq = QK_ROPE // 4
        inv = ROPE_THETA ** (-jnp.arange(n_freq, dtype=F32) / n_freq)
        ang = jnp.stack([row[:, None] * inv, col[:, None] * inv], axis=1)
        return jnp.cos(ang), jnp.sin(ang)

    cos, sin = tables(LAT_T)
    cos64 = jnp.concatenate([cos[:, 0], cos[:, 0], cos[:, 1], cos[:, 1]], axis=1)
    sin64 = jnp.concatenate([-sin[:, 0], sin[:, 0], -sin[:, 1], sin[:, 1]], axis=1)
    cos_all = jnp.concatenate([jnp.ones((N_CTX, QK_ROPE), F32)] + [cos64] * N_LAT_SEQ, axis=0)
    sin_all = jnp.concatenate([jnp.zeros((N_CTX, QK_ROPE), F32)] + [sin64] * N_LAT_SEQ, axis=0)
    return jnp.tile(cos_all, (1, 2)), jnp.tile(sin_all, (1, 2))


_PAIR_SWAP = np.concatenate([np.arange(16, 32), np.arange(0, 16), np.arange(48, 64), np.arange(32, 48)])


def _mla_weights(w_in, w_uq, w_ukv):
    kpe = w_in[:, Q_LORA + KV_LORA:]
    pad = jnp.zeros((D, 64), F32)
    w_in2 = jnp.concatenate([w_in[:, :Q_LORA + KV_LORA], kpe, pad, kpe[:, _PAIR_SWAP], pad], axis=1)
    uq = w_uq.reshape(Q_LORA, MLA_HEADS, QK_NOPE + QK_ROPE)
    rope = uq[:, :, QK_NOPE:]
    w_uq2 = jnp.concatenate([uq[:, :, :QK_NOPE].reshape(Q_LORA, NOPE_ALL), rope.reshape(Q_LORA, ROPE_ALL),
                             rope[:, :, _PAIR_SWAP].reshape(Q_LORA, ROPE_ALL)], axis=1)
    ukv = w_ukv.reshape(KV_LORA, MLA_HEADS, QK_NOPE + V_HEAD)
    w_ukv2 = jnp.concatenate([ukv[:, :, :QK_NOPE].reshape(KV_LORA, NOPE_ALL),
                              ukv[:, :, QK_NOPE:].reshape(KV_LORA, NOPE_ALL)], axis=1)
    return w_in2.astype(BF16), w_uq2.astype(BF16), w_ukv2.astype(BF16)


def _ml_init_state(state_c, state_n, state_m, j, direction):
    c = state_c[:, j, direction].astype(F32)
    n = state_n[:, j, direction].astype(F32)[..., None]
    m = jnp.broadcast_to(state_m[:, j, direction].astype(F32)[..., None, None], n.shape)
    pad = jnp.zeros(c.shape[:-1] + (ML_AUG - ML_DV - 2,), F32)
    lat = jnp.concatenate([c, n, m, pad], axis=-1)
    return jnp.concatenate([jnp.zeros((N_CTX_SEQ,) + lat.shape[1:], F32), lat], axis=0)


def kernel(x_prompt, x_sample, cache_mla_ckv, cache_mla_kpe, state_mlstm_C, state_mlstm_n, state_mlstm_m, c, c_ctx, norm1_g, norm2_g, ada_w, ada_b, ml_w_in, ml_gate_b, ml_norm_g, ml_w_out, mla_w_in, mla_q_norm_g, mla_w_uq, mla_kv_norm_g, mla_w_ukv, mla_w_out, fn_w_out, router_w, router_b, exp_w1, exp_b1, exp_w2, exp_b2, final_g):
    x = jnp.concatenate([x_prompt.reshape(N_CTX, D), x_sample.reshape(N_LAT_SEQ * LAT_T, D)], axis=0)

    cond8 = jnp.concatenate([c_ctx[None, :], c, jnp.zeros((8 - 1 - N_LAT_SEQ, D), F32)], axis=0)
    mod = _modulation(cond8, ada_w, ada_b)
    tile_cond = np.concatenate([np.zeros(CTX_TILES, np.int32)] +
                               [np.full(LAT_TILES, 1 + s, np.int32) for s in range(N_LAT_SEQ)])
    modt = mod[:, tile_cond].reshape(DEPTH, N_TILES, 6, D)

    hk = ML_HEADS * ML_DK
    states = []
    new_ckv = new_kpe = None
    for l in range(DEPTH):
        kind, j = l % 3, l // 3
        n1 = norm1_g[l].reshape(1, D)
        if kind == 0:
            w = ml_w_in[j]
            w_gates = w[:, 2 * hk + 2 * D:]
            qkv, o, gg, gt = _ml_in(x, modt[l], n1, w[:, :2 * hk + 2 * D].astype(BF16),
                                    jnp.pad(w_gates, ((0, 0), (0, 128 - 4 * ML_HEADS))).astype(BF16),
                                    w_gates.T.astype(BF16))
            gb = ml_gate_b[j].reshape(4 * ML_HEADS).astype(F32)
            gb_row = jnp.pad(gb, (0, 128 - 4 * ML_HEADS)).reshape(1, 128)
            gb_col = jnp.broadcast_to(gb[:, None], (4 * ML_HEADS, 128))
            hf, hb, st_f, st_b = _ml_core(qkv, gg, gt, gb_row, gb_col,
                                          _ml_init_state(state_mlstm_C, state_mlstm_n, state_mlstm_m, j, 0),
                                          _ml_init_state(state_mlstm_C, state_mlstm_n, state_mlstm_m, j, 1))
            states.append((st_f[:N_CTX_SEQ], st_b[:N_CTX_SEQ]))
            x = _ml_out(hf, hb, o, ml_norm_g[j].reshape(1, D), ml_w_out[j].astype(BF16), x, modt[l])
        elif kind == 1:
            w_in2, w_uq2, w_ukv2 = _mla_weights(mla_w_in[j], mla_w_uq[j], mla_w_ukv[j])
            cos2, sin2 = _rope_tables()
            qn, qr, ckv, kpe = _mla_in(x, modt[l], n1, w_in2, mla_q_norm_g[j].reshape(1, Q_LORA),
                                       mla_kv_norm_g[j].reshape(1, KV_LORA), w_uq2, cos2, sin2)
            new_ckv, new_kpe = ckv[:N_CTX], kpe[:N_CTX]
            kn_tok, v_tok = _kv_expand(ckv, w_ukv2)
            kn_past, v_past = _kv_expand(cache_mla_ckv[:, j].reshape(N_LAT_SEQ * PAST_LEN, KV_LORA), w_ukv2)
            kr_past = cache_mla_kpe[:, j].reshape(N_LAT_SEQ * PAST_LEN, QK_ROPE)
            att = _attention(qn, qr, kn_tok, v_tok, kpe, kn_past, v_past, kr_past)
            x = _res_linear(att, mla_w_out[j].astype(BF16), x, modt[l])
        else:
            x = _fourier(x, modt[l], n1, _dft_tables(), fn_w_out[j].astype(BF16))
        x = _moe(l, x, modt[l], norm2_g[l].reshape(1, D), router_w[l], router_b[l],
                 exp_w1, exp_b1, exp_w2, exp_b2, final_g.reshape(1, D), l == DEPTH - 1)

    y_prompt = x[:N_CTX].reshape(N_CTX_SEQ, CTX_T, D)
    y_sample = x[N_CTX:].reshape(N_LAT_SEQ, LAT_T, D)
    new_mla_ckv = new_ckv.reshape(N_CTX_SEQ, 1, CTX_T, KV_LORA)
    new_mla_kpe = new_kpe.reshape(N_CTX_SEQ, 1, CTX_T, QK_ROPE)
    st = jnp.stack([jnp.stack([sf, sb], axis=1) for sf, sb in states], axis=1)
    new_c = st[..., 0:ML_DV]
    new_n = st[..., ML_N_LANE]
    new_m = st[..., 0, ML_M_LANE]
    return (y_prompt, y_sample, new_mla_ckv, new_mla_kpe, new_c, new_n, new_m)
```

```python
import functools

import numpy as np
import jax
import jax.numpy as jnp
from jax import lax
from jax.experimental import pallas as pl
from jax.experimental.pallas import tpu as pltpu

F32 = jnp.float32
BF16 = jnp.bfloat16

D = 1024
DEPTH = 4
N_CTX_SEQ, CTX_T = 16, 256
N_LAT_SEQ, LAT_T = 2, 2048
N_CTX = N_CTX_SEQ * CTX_T
N_TOK = N_CTX + N_LAT_SEQ * LAT_T
TM = 256
N_TILES = N_TOK // TM
CTX_TILES = N_CTX // TM
LAT_TILES = LAT_T // TM
N_SEQ = N_CTX_SEQ + N_LAT_SEQ
PAST_LEN = 512
GRID_W = 64
RMS_EPS = 1e-6

ML_HEADS, ML_DK, ML_DV = 8, 64, 128
ML_AUG = 256
ML_N_LANE, ML_M_LANE = ML_DV, ML_DV + 1

MLA_HEADS, QK_NOPE, QK_ROPE, V_HEAD = 16, 128, 64, 128
Q_LORA, KV_LORA = 384, 256
ROPE_THETA = 10000.0
ATT_HG = 4
ATT_SCALE = (QK_NOPE + QK_ROPE) ** -0.5

FN_GROUPS, FN_GW = 4, 256

N_EXPERTS, TOP_K = 32, 4
SWIGLU_ALPHA, SWIGLU_LIMIT = 1.702, 7.0
MOE_BM = 256
N_ASSIGN = N_TOK * TOP_K
MOE_NB = N_ASSIGN // MOE_BM + N_EXPERTS
MOE_STEPS = MOE_NB + 1
MOE_SLOTS = 3
MOE_HC = 256
MOE_NHC = D // MOE_HC
MOE_RPC = MOE_BM // MOE_NHC
Y4_ROWS = N_ASSIGN + MOE_SLOTS * MOE_BM

VMEM_LIMIT = 56 * 1024 * 1024


def _cp(n_grid_axes, vmem=None, **kw):
    return pltpu.CompilerParams(dimension_semantics=("arbitrary",) * n_grid_axes, vmem_limit_bytes=vmem, **kw)


def _dot(a, b):
    return jnp.dot(a, b, preferred_element_type=F32)


def _dot_nt(a, b):
    return lax.dot_general(a, b, (((1,), (1,)), ((), ())), preferred_element_type=F32)


def _dot_tn(a, b):
    return lax.dot_general(a, b, (((0,), (0,)), ((), ())), preferred_element_type=F32)


def _sigmoid(x):
    return 1.0 / (1.0 + jnp.exp(-x))


def _log_sigmoid(x):
    return jnp.minimum(x, 0.0) - jnp.log(1.0 + jnp.exp(-jnp.abs(x)))


def _rms(x):
    return x * lax.rsqrt(jnp.mean(x * x, axis=-1, keepdims=True) + RMS_EPS)


def _norm_mod(x, g, shift, scale):
    return (_rms(x) * g) * (1.0 + scale) + shift


def _split2(x):
    hi = x.astype(BF16)
    lo = (x - hi.astype(F32)).astype(BF16)
    return hi, lo


def _split3(x):
    hi = x.astype(BF16)
    r = x - hi.astype(F32)
    mid = r.astype(BF16)
    lo = (r - mid.astype(F32)).astype(BF16)
    return hi, mid, lo


LANE_CHUNKS = D // 128


def _store_token_tiles(ref, row0, val):
    n = val.shape[0]
    for c in range(LANE_CHUNKS):
        ref[pl.ds(row0 * LANE_CHUNKS + c, n, stride=LANE_CHUNKS), :] = val[:, c * 128:(c + 1) * 128]


def _load_token_tile_chunk(ref, row0, n, c):
    return ref[pl.ds(row0 * LANE_CHUNKS + c, n, stride=LANE_CHUNKS), :]


PACK_CHUNKS = LANE_CHUNKS // 2
HI16 = 0xFFFF0000


def _store_packed_rows(ref, val):
    n = val.shape[0]
    for c in range(PACK_CHUNKS):
        lo = pltpu.bitcast(val[:, c * 128:(c + 1) * 128].astype(BF16).astype(F32), jnp.uint32)
        hi = pltpu.bitcast(val[:, D // 2 + c * 128:D // 2 + (c + 1) * 128].astype(BF16).astype(F32), jnp.uint32)
        ref[pl.ds(c, n, stride=PACK_CHUNKS), :] = (hi & jnp.uint32(HI16)) | (lo >> jnp.uint32(16))


def _load_packed_rows(ref, row0, n, out_ref):
    for c in range(PACK_CHUNKS):
        u = ref[pl.ds(row0 * PACK_CHUNKS + c, n, stride=PACK_CHUNKS), :]
        out_ref[:, c * 128:(c + 1) * 128] = pltpu.bitcast(u << jnp.uint32(16), F32).astype(BF16)
        out_ref[:, D // 2 + c * 128:D // 2 + (c + 1) * 128] = pltpu.bitcast(u & jnp.uint32(HI16), F32).astype(BF16)


def _tile_seq(r):
    return jnp.where(r < CTX_TILES, r, CTX_TILES + (r - CTX_TILES) // LAT_TILES)


def _lat_seq(r):
    return jnp.clip((r - CTX_TILES) // LAT_TILES, 0, N_LAT_SEQ - 1)


MOD_TN = 1536


def _mod_kernel(c_ref, w_ref, b_ref, o_ref):
    a = c_ref[...]
    s_hi, s_lo = _split2(a * _sigmoid(a))
    w_hi, w_lo = _split2(w_ref[0])
    o_ref[0] = _dot_split(s_hi, s_lo, w_hi, w_lo) + b_ref[0]


def _modulation(cond8, ada_w, ada_b):
    n_col = ada_w.shape[-1]
    return pl.pallas_call(
        _mod_kernel,
        out_shape=jax.ShapeDtypeStruct((DEPTH, 8, n_col), F32),
        grid=(DEPTH, n_col // MOD_TN),
        in_specs=[pl.BlockSpec((8, D), lambda l, j: (0, 0)),
                  pl.BlockSpec((1, D, MOD_TN), lambda l, j: (l, 0, j)),
                  pl.BlockSpec((1, 1, MOD_TN), lambda l, j: (l, 0, j))],
        out_specs=pl.BlockSpec((1, 8, MOD_TN), lambda l, j: (l, 0, j)),
        compiler_params=_cp(2, VMEM_LIMIT),
    )(cond8, ada_w, ada_b.reshape(DEPTH, 1, n_col))


def _ml_in_kernel(x_ref, mod_ref, g_ref, w_ref, wg_ref, wgt_ref, qkv_ref, o_ref, gg_ref, gt_ref):
    h = _norm_mod(x_ref[...], g_ref[...], mod_ref[0, 0:1, :], mod_ref[0, 1:2, :]).astype(BF16)
    hk = ML_HEADS * ML_DK
    lane = lax.broadcasted_iota(jnp.int32, (1, 2 * hk), 1)
    qscale = jnp.where(lane < hk, ML_DK ** -0.5, 1.0).astype(F32)
    qkv_ref[:, 0:2 * hk] = (_dot(h, w_ref[:, 0:2 * hk]) * qscale).astype(BF16)
    qkv_ref[:, 2 * hk:2 * hk + D] = _dot(h, w_ref[:, 2 * hk:2 * hk + D]).astype(BF16)
    o_ref[...] = _dot(h, w_ref[:, 2 * hk + D:2 * hk + 2 * D])
    gg_ref[...] = _dot(h, wg_ref[...])
    gt_ref[...] = _dot_nt(wgt_ref[...], h)


def _ml_in(x, modt, norm_g, w_main, w_g, w_gt):
    tile = lambda i: (i, 0)
    const = lambda i: (0, 0)
    return pl.pallas_call(
        _ml_in_kernel,
        out_shape=(jax.ShapeDtypeStruct((N_TOK, 2 * D), BF16),
                   jax.ShapeDtypeStruct((N_TOK, D), F32),
                   jax.ShapeDtypeStruct((N_TOK, 128), F32),
                   jax.ShapeDtypeStruct((32, N_TOK), F32)),
        grid=(N_TILES,),
        in_specs=[pl.BlockSpec((TM, D), tile),
                  pl.BlockSpec((1, 6, D), lambda i: (i, 0, 0)),
                  pl.BlockSpec((1, D), const),
                  pl.BlockSpec((D, 3 * D), const),
                  pl.BlockSpec((D, 128), const),
                  pl.BlockSpec((32, D), const)],
        out_specs=(pl.BlockSpec((TM, 2 * D), tile),
                   pl.BlockSpec((TM, D), tile),
                   pl.BlockSpec((TM, 128), tile),
                   pl.BlockSpec((32, TM), lambda i: (0, i))),
        compiler_params=_cp(1, VMEM_LIMIT),
    )(x, modt, norm_g, w_main, w_g, w_gt)


def _ml_direction(lower, q_ref, k_ref, v_ref, g_ref, gt_ref, gbrow_ref, gbcol_ref, h_ref, st_ref, c_s, m_s):
    L = TM
    d_off = 0 if lower else ML_HEADS
    row = lax.broadcasted_iota(jnp.int32, (L, L), 0)
    col = lax.broadcasted_iota(jnp.int32, (L, L), 1)
    mask = (col <= row) if lower else (col >= row)
    tri = jnp.where(mask, 1.0, 0.0).astype(BF16)

    lane = lax.broadcasted_iota(jnp.int32, (L, 128), 1)
    gc = g_ref[...] + gbrow_ref[...]
    gc = jnp.where((lane >= 2 * ML_HEADS) & (lane < 4 * ML_HEADS), _log_sigmoid(gc), gc)
    bc = sum(_dot(tri, p) for p in _split3(gc))
    sub = lax.broadcasted_iota(jnp.int32, (32, L), 0)
    gr = gt_ref[...] + gbcol_ref[:, 0:1]
    gr = jnp.where(sub >= 2 * ML_HEADS, _log_sigmoid(gr), gr)
    br = sum(_dot_nt(p, tri) for p in _split3(gr))

    q = q_ref[...]
    k = k_ref[...]
    v = v_ref[...]
    lane_a = lax.broadcasted_iota(jnp.int32, (L, ML_AUG - ML_DV), 1)
    ones_blk = jnp.where(lane_a == 0, 1.0, 0.0).astype(BF16)
    lane_s = lax.broadcasted_iota(jnp.int32, (ML_DK, ML_AUG), 1)

    for hd in range(ML_HEADS):
        ci = d_off + hd
        cf = 2 * ML_HEADS + d_off + hd
        i_col, b_col = gc[:, ci:ci + 1], bc[:, cf:cf + 1]
        i_row, b_row = gr[ci:ci + 1, :], br[cf:cf + 1, :]
        total = b_col[L - 1:L, :] if lower else b_col[0:1, :]
        m = m_s[hd][0:1, 0:1]
        c_aug = c_s[hd]
        dmat = jnp.where(mask, (b_col - b_row) + i_row, -jnp.inf)
        mi = jnp.max(dmat, axis=1, keepdims=True)
        a_col = b_col + m
        m_row = jnp.maximum(a_col, mi)
        qh = q[:, hd * ML_DK:(hd + 1) * ML_DK]
        kh = k[:, hd * ML_DK:(hd + 1) * ML_DK]
        v_aug = jnp.concatenate([v[:, hd * ML_DV:(hd + 1) * ML_DV], ones_blk], axis=1)
        p = (_dot_nt(qh, kh) * jnp.exp(dmat - mi)).astype(BF16)
        num = jnp.exp(mi - m_row) * _dot(p, v_aug) + jnp.exp(a_col - m_row) * _dot(qh, c_aug.astype(BF16))
        den = jnp.maximum(jnp.abs(num[:, ML_N_LANE:ML_N_LANE + 1]), jnp.exp(-m_row))
        h_ref[:, hd * ML_DV:(hd + 1) * ML_DV] = num[:, 0:ML_DV] / den

        g_col = (total - b_col) + i_col
        m_new = jnp.maximum(total + m, jnp.max(g_col, axis=0, keepdims=True))
        kw = (kh.astype(F32) * jnp.exp(g_col - m_new)).astype(BF16)
        c_new = jnp.exp(total + m - m_new) * c_aug + _dot_tn(kw, v_aug)
        c_s[hd] = c_new
        m_s[hd] = jnp.broadcast_to(m_new, (8, 128))
        st_ref[0, hd] = jnp.where(lane_s == ML_M_LANE, m_new, c_new)


def _ml_core_kernel(qf, kf, vf, gf, gtf, qb, kb, vb, gb, gtb, gbrow, gbcol, initf, initb,
                    hf_ref, hb_ref, stf_ref, stb_ref, cf_s, cb_s, mf_s, mb_s):
    i = pl.program_id(0)
    start_f = (i <= CTX_TILES) | (i == CTX_TILES + LAT_TILES)
    start_b = (i == 0) | (i == LAT_TILES) | (i >= 2 * LAT_TILES)

    def load_state(init_ref, c_s, m_s):
        c_s[...] = init_ref[0]
        for hd in range(ML_HEADS):
            m_s[hd] = jnp.broadcast_to(init_ref[0, hd][0:1, ML_M_LANE:ML_M_LANE + 1], (8, 128))

    @pl.when(start_f)
    def _():
        load_state(initf, cf_s, mf_s)

    @pl.when(start_b)
    def _():
        load_state(initb, cb_s, mb_s)

    _ml_direction(True, qf, kf, vf, gf, gtf, gbrow, gbcol, hf_ref, stf_ref, cf_s, mf_s)
    _ml_direction(False, qb, kb, vb, gb, gtb, gbrow, gbcol, hb_ref, stb_ref, cb_s, mb_s)


def _ml_core(qkv, gg, gt, gb_row, gb_col, init_f, init_b):
    last = N_TILES - 1
    fwd = lambda i: i
    bwd = lambda i: last - i
    hk = ML_HEADS * ML_DK

    def specs(t):
        return [pl.BlockSpec((TM, hk), lambda i: (t(i), 0)),
                pl.BlockSpec((TM, hk), lambda i: (t(i), 1)),
                pl.BlockSpec((TM, D), lambda i: (t(i), 1)),
                pl.BlockSpec((TM, 128), lambda i: (t(i), 0)),
                pl.BlockSpec((32, TM), lambda i: (0, t(i)))]

    def st_spec(t):
        return pl.BlockSpec((1, ML_HEADS, ML_DK, ML_AUG), lambda i: (_tile_seq(t(i)), 0, 0, 0))

    st_shape = jax.ShapeDtypeStruct((N_SEQ, ML_HEADS, ML_DK, ML_AUG), F32)
    return pl.pallas_call(
        _ml_core_kernel,
        out_shape=(jax.ShapeDtypeStruct((N_TOK, D), F32), jax.ShapeDtypeStruct((N_TOK, D), F32),
                   st_shape, st_shape),
        grid=(N_TILES,),
        in_specs=specs(fwd) + specs(bwd) + [
            pl.BlockSpec((1, 128), lambda i: (0, 0)),
            pl.BlockSpec((32, 128), lambda i: (0, 0)),
            st_spec(fwd), st_spec(bwd)],
        out_specs=(pl.BlockSpec((TM, D), lambda i: (i, 0)),
                   pl.BlockSpec((TM, D), lambda i: (last - i, 0)),
                   st_spec(fwd), st_spec(bwd)),
        scratch_shapes=[pltpu.VMEM((ML_HEADS, ML_DK, ML_AUG), F32),
                        pltpu.VMEM((ML_HEADS, ML_DK, ML_AUG), F32),
                        pltpu.VMEM((ML_HEADS, 8, 128), F32),
                        pltpu.VMEM((ML_HEADS, 8, 128), F32)],
        compiler_params=_cp(1, VMEM_LIMIT),
    )(qkv, qkv, qkv, gg, gt, qkv, qkv, qkv, gg, gt, gb_row, gb_col, init_f, init_b)


def _ml_out_kernel(hf_ref, hb_ref, o_ref, ng_ref, w_ref, x_ref, mod_ref, out_ref, z_s):
    hh = hf_ref[...] + hb_ref[...]
    o = o_ref[...]
    for hd in range(ML_HEADS):
        sl = slice(hd * ML_DV, (hd + 1) * ML_DV)
        z_s[:, sl] = (_sigmoid(o[:, sl]) * (_rms(hh[:, sl]) * ng_ref[:, sl])).astype(BF16)
    out_ref[...] = x_ref[...] + mod_ref[0, 2:3, :] * _dot(z_s[...], w_ref[...])


def _ml_out(hf, hb, o, norm_g, w_out, x, modt):
    tile = lambda i: (i, 0)
    const = lambda i: (0, 0)
    return pl.pallas_call(
        _ml_out_kernel,
        out_shape=jax.ShapeDtypeStruct((N_TOK, D), F32),
        grid=(N_TILES,),
        in_specs=[pl.BlockSpec((TM, D), tile), pl.BlockSpec((TM, D), tile), pl.BlockSpec((TM, D), tile),
                  pl.BlockSpec((1, D), const), pl.BlockSpec((D, D), const),
                  pl.BlockSpec((TM, D), tile), pl.BlockSpec((1, 6, D), lambda i: (i, 0, 0))],
        out_specs=pl.BlockSpec((TM, D), tile),
        scratch_shapes=[pltpu.VMEM((TM, D), BF16)],
        compiler_params=_cp(1, VMEM_LIMIT),
    )(hf, hb, o, norm_g, w_out, x, modt)


MLA_IN_COLS = 896
NOPE_ALL = MLA_HEADS * QK_NOPE
ROPE_ALL = MLA_HEADS * QK_ROPE


QK_CAT = 256
QCAT_ALL = MLA_HEADS * QK_CAT
ROPE_PAD_ALL = MLA_HEADS * 128


def _mla_in_kernel(x_ref, mod_ref, g_ref, w_ref, qg_ref, kvg_ref, wuq_ref, cos_ref, sin_ref,
                   q_ref, ckv_ref, kpe_ref):
    h = _norm_mod(x_ref[...], g_ref[...], mod_ref[0, 0:1, :], mod_ref[0, 1:2, :]).astype(BF16)
    c = _dot(h, w_ref[...])
    cos, sin = cos_ref[...], sin_ref[...]
    ckv_ref[...] = _rms(c[:, Q_LORA:Q_LORA + KV_LORA]) * kvg_ref[...]
    kpe_ref[...] = c[:, 640:768] * cos + c[:, 768:896] * sin
    cq = (_rms(c[:, 0:Q_LORA]) * qg_ref[...]).astype(BF16)
    nope = _dot(cq, wuq_ref[:, 0:NOPE_ALL])
    rope = _dot(cq, wuq_ref[:, NOPE_ALL:NOPE_ALL + ROPE_PAD_ALL])
    swapped = _dot(cq, wuq_ref[:, NOPE_ALL + ROPE_PAD_ALL:NOPE_ALL + 2 * ROPE_PAD_ALL])
    for hd in range(MLA_HEADS):
        sl = slice(hd * 128, (hd + 1) * 128)
        q_ref[:, hd * QK_CAT:hd * QK_CAT + QK_NOPE] = nope[:, sl].astype(BF16)
        q_ref[:, hd * QK_CAT + QK_NOPE:(hd + 1) * QK_CAT] = (rope[:, sl] * cos + swapped[:, sl] * sin).astype(BF16)


def _mla_in(x, modt, norm_g, w_in2, qg, kvg, w_uq2, cos2, sin2):
    tile = lambda i: (i, 0)
    const = lambda i: (0, 0)
    return pl.pallas_call(
        _mla_in_kernel,
        out_shape=(jax.ShapeDtypeStruct((N_TOK, QCAT_ALL), BF16),
                   jax.ShapeDtypeStruct((N_TOK, KV_LORA), F32),
                   jax.ShapeDtypeStruct((N_TOK, 128), F32)),
        grid=(N_TILES,),
        in_specs=[pl.BlockSpec((TM, D), tile), pl.BlockSpec((1, 6, D), lambda i: (i, 0, 0)),
                  pl.BlockSpec((1, D), const), pl.BlockSpec((D, MLA_IN_COLS), const),
                  pl.BlockSpec((1, Q_LORA), const), pl.BlockSpec((1, KV_LORA), const),
                  pl.BlockSpec((Q_LORA, NOPE_ALL + 2 * ROPE_PAD_ALL), const),
                  pl.BlockSpec((TM, 128), tile), pl.BlockSpec((TM, 128), tile)],
        out_specs=(pl.BlockSpec((TM, QCAT_ALL), tile),
                   pl.BlockSpec((TM, KV_LORA), tile), pl.BlockSpec((TM, 128), tile)),
        compiler_params=_cp(1, VMEM_LIMIT),
    )(x, modt, norm_g, w_in2, qg, kvg, w_uq2, cos2, sin2)


def _kv_expand_kernel(c_ref, kr_ref, w_ref, k_ref, v_ref):
    c = c_ref[...].astype(BF16)
    kn = _dot(c, w_ref[:, 0:NOPE_ALL])
    kr = kr_ref[...].astype(BF16)
    for hd in range(MLA_HEADS):
        k_ref[:, hd * QK_CAT:hd * QK_CAT + QK_NOPE] = kn[:, hd * QK_NOPE:(hd + 1) * QK_NOPE].astype(BF16)
        k_ref[:, hd * QK_CAT + QK_NOPE:(hd + 1) * QK_CAT] = kr
    v_ref[...] = _dot(c, w_ref[:, NOPE_ALL:2 * NOPE_ALL]).astype(BF16)


def _kv_expand(ckv, kr, w_ukv2):
    rows = ckv.shape[0]
    tile = lambda i: (i, 0)
    return pl.pallas_call(
        _kv_expand_kernel,
        out_shape=(jax.ShapeDtypeStruct((rows, QCAT_ALL), BF16), jax.ShapeDtypeStruct((rows, NOPE_ALL), BF16)),
        grid=(rows // TM,),
        in_specs=[pl.BlockSpec((TM, KV_LORA), tile), pl.BlockSpec((TM, 128), tile),
                  pl.BlockSpec((KV_LORA, 2 * NOPE_ALL), lambda i: (0, 0))],
        out_specs=(pl.BlockSpec((TM, QCAT_ALL), tile), pl.BlockSpec((TM, NOPE_ALL), tile)),
        compiler_params=_cp(1, VMEM_LIMIT),
    )(ckv, kr, w_ukv2)


def _attend(q_ref, segs, out_ref):
    for hd in range(ATT_HG):
        ksl = slice(hd * QK_CAT, (hd + 1) * QK_CAT)
        vsl = slice(hd * V_HEAD, (hd + 1) * V_HEAD)
        q = q_ref[:, ksl]
        scores = [_dot_nt(q, k_ref[:, ksl]) * ATT_SCALE for k_ref, _ in segs]
        m = functools.reduce(jnp.maximum, [jnp.max(s, axis=1, keepdims=True) for s in scores])
        ps = [jnp.exp(s - m) for s in scores]
        denom = sum(jnp.sum(p, axis=1, keepdims=True) for p in ps)
        acc = sum(_dot(p.astype(BF16), v_ref[:, vsl]) for p, (_, v_ref) in zip(ps, segs))
        out_ref[:, vsl] = (acc / denom).astype(BF16)


def _attn_kernel(q_ref, kc_ref, vc_ref, kl_ref, vl_ref, kp_ref, vp_ref, out_ref):
    r = pl.program_id(1)

    @pl.when(r < CTX_TILES)
    def _():
        _attend(q_ref, [(kc_ref, vc_ref)], out_ref)

    @pl.when(r >= CTX_TILES)
    def _():
        _attend(q_ref, [(kp_ref, vp_ref), (kl_ref, vl_ref)], out_ref)


def _attention(q, k_tok, v_tok, k_past, v_past):
    kw, vw = ATT_HG * QK_CAT, ATT_HG * V_HEAD
    ctx_t = lambda g, r: (jnp.minimum(r, CTX_TILES - 1), g)
    lat_t = lambda g, r: (N_CTX // LAT_T + _lat_seq(r), g)
    past_t = lambda g, r: (_lat_seq(r), g)
    return pl.pallas_call(
        _attn_kernel,
        out_shape=jax.ShapeDtypeStruct((N_TOK, NOPE_ALL), BF16),
        grid=(MLA_HEADS // ATT_HG, N_TILES),
        in_specs=[pl.BlockSpec((TM, kw), lambda g, r: (r, g)),
                  pl.BlockSpec((CTX_T, kw), ctx_t), pl.BlockSpec((CTX_T, vw), ctx_t),
                  pl.BlockSpec((LAT_T, kw), lat_t), pl.BlockSpec((LAT_T, vw), lat_t),
                  pl.BlockSpec((PAST_LEN, kw), past_t), pl.BlockSpec((PAST_LEN, vw), past_t)],
        out_specs=pl.BlockSpec((TM, vw), lambda g, r: (r, g)),
        compiler_params=_cp(2, VMEM_LIMIT),
    )(q, k_tok, v_tok, k_tok, v_tok, k_past, v_past)


def _res_linear_kernel(a_ref, w_ref, x_ref, mod_ref, out_ref):
    out_ref[...] = x_ref[...] + mod_ref[0, 2:3, :] * _dot(a_ref[...], w_ref[...])


def _res_linear(a, w, x, modt):
    k = a.shape[1]
    tile = lambda i: (i, 0)
    return pl.pallas_call(
        _res_linear_kernel,
        out_shape=jax.ShapeDtypeStruct((N_TOK, D), F32),
        grid=(N_TILES,),
        in_specs=[pl.BlockSpec((TM, k), tile), pl.BlockSpec((k, D), lambda i: (0, 0)),
                  pl.BlockSpec((TM, D), tile), pl.BlockSpec((1, 6, D), lambda i: (i, 0, 0))],
        out_specs=pl.BlockSpec((TM, D), tile),
        compiler_params=_cp(1, VMEM_LIMIT),
    )(a, w, x, modt)


def _dot_split(a_hi, a_lo, b_hi, b_lo):
    return _dot(a_hi, b_hi) + (_dot(a_hi, b_lo) + _dot(a_lo, b_hi))


def _fn_channel_dft(x, g, shift, scale, wc_hi, wc_lo):
    h_hi, h_lo = _split2(_norm_mod(x, g, shift, scale))
    a_parts, b_parts = [], []
    for grp in range(FN_GROUPS):
        sl = slice(grp * FN_GW, (grp + 1) * FN_GW)
        ab = _dot_split(h_hi[:, sl], h_lo[:, sl], wc_hi, wc_lo)
        a_parts.append(ab[:, 0:FN_GW])
        b_parts.append(ab[:, FN_GW:2 * FN_GW])
    return jnp.concatenate(a_parts, axis=1), jnp.concatenate(b_parts, axis=1)


FN_STEPS = CTX_TILES + N_LAT_SEQ * 2 * LAT_TILES


def _fn_step(t):
    u = jnp.maximum(t - CTX_TILES, 0)
    seq, ph, tile = u // (2 * LAT_TILES), (u % (2 * LAT_TILES)) // LAT_TILES, u % LAT_TILES
    is_ctx = t < CTX_TILES
    return is_ctx, ph, tile, jnp.where(is_ctx, t, CTX_TILES + seq * LAT_TILES + tile)


def _fn_kernel(x_ref, mod_ref, g_ref, wc_ref, tc_ref, tl_ref, w_ref, out_ref, ab_hi_s, ab_lo_s):
    is_ctx, ph, tile, _ = _fn_step(pl.program_id(0))
    g = g_ref[...]
    shift, scale, gate = mod_ref[0, 0:1, :], mod_ref[0, 1:2, :], mod_ref[0, 2:3, :]
    wc_hi, wc_lo = _split2(wc_ref[...])

    def finish(f, t_len):
        f = f * ((t_len * FN_GW) ** -0.5)
        out_ref[...] = x_ref[...] + gate * _dot(f.astype(BF16), w_ref[...])

    @pl.when(is_ctx)
    def _():
        a, b = _fn_channel_dft(x_ref[...], g, shift, scale, wc_hi, wc_lo)
        ab_hi, ab_lo = _split2(jnp.concatenate([a, b], axis=0))
        finish(_dot_split(*_split2(tc_ref[...]), ab_hi, ab_lo), CTX_T)

    @pl.when(jnp.logical_not(is_ctx) & (ph == 0))
    def _():
        a, b = _fn_channel_dft(x_ref[...], g, shift, scale, wc_hi, wc_lo)
        a_hi, a_lo = _split2(a)
        b_hi, b_lo = _split2(b)
        rows = pl.ds(pl.multiple_of(tile * TM, TM), TM)
        rows_b = pl.ds(pl.multiple_of(LAT_T + tile * TM, TM), TM)
        ab_hi_s[rows, :] = a_hi
        ab_lo_s[rows, :] = a_lo
        ab_hi_s[rows_b, :] = b_hi
        ab_lo_s[rows_b, :] = b_lo

    @pl.when(jnp.logical_not(is_ctx) & (ph == 1))
    def _():
        finish(_dot_split(*_split2(tl_ref[...]), ab_hi_s[...], ab_lo_s[...]), LAT_T)


def _fourier(x, modt, norm_g, tabs, w_out):
    wc, tc, tl = tabs
    const = lambda t: (0, 0)
    x_tile = lambda t: (_fn_step(t)[3], 0)

    def out_tile(t):
        is_ctx, ph, tile, gt = _fn_step(t)
        return (jnp.where(is_ctx | (ph == 1), gt, gt - tile), 0)

    def tab_row(t):
        is_ctx, ph, tile, _ = _fn_step(t)
        return (jnp.where(is_ctx | (ph == 0), 0, tile), 0)

    return pl.pallas_call(
        _fn_kernel,
        out_shape=jax.ShapeDtypeStruct((N_TOK, D), F32),
        grid=(FN_STEPS,),
        in_specs=[pl.BlockSpec((TM, D), x_tile),
                  pl.BlockSpec((1, 6, D), lambda t: (_fn_step(t)[3], 0, 0)),
                  pl.BlockSpec((1, D), const),
                  pl.BlockSpec((FN_GW, 2 * FN_GW), const),
                  pl.BlockSpec((CTX_T, 2 * CTX_T), const),
                  pl.BlockSpec((TM, 2 * LAT_T), tab_row),
                  pl.BlockSpec((D, D), const)],
        out_specs=pl.BlockSpec((TM, D), out_tile),
        scratch_shapes=[pltpu.VMEM((2 * LAT_T, D), BF16), pltpu.VMEM((2 * LAT_T, D), BF16)],
        compiler_params=_cp(1, VMEM_LIMIT),
    )(x, modt, norm_g, wc, tc, tl, w_out)


@functools.lru_cache(maxsize=None)
def _dft_tables():
    def cos_sin(n):
        k = np.arange(n, dtype=np.int64)
        ang = ((k[:, None] * k[None, :]) % n).astype(np.float64) * (2.0 * np.pi / n)
        return np.cos(ang), np.sin(ang)

    cc, sc = cos_sin(FN_GW)
    out = [np.concatenate([cc, sc], axis=1).astype(np.float32)]
    for t_len in (CTX_T, LAT_T):
        ct, st = cos_sin(t_len)
        out.append(np.concatenate([ct, -st], axis=1).astype(np.float32))
    return tuple(out)


def _router_kernel(x_ref, mod_ref, g_ref, rw_hi_ref, rw_lo_ref, rb_ref, h_ref, idt_ref, gate_ref, cnt_ref):
    h = _norm_mod(x_ref[...], g_ref[...], mod_ref[0, 3:4, :], mod_ref[0, 4:5, :])
    _store_packed_rows(h_ref, h)
    h_hi, h_lo = _split2(h)
    logits = _dot_split(h_hi, h_lo, rw_hi_ref[...], rw_lo_ref[...]) + rb_ref[...]
    lane = lax.broadcasted_iota(jnp.int32, logits.shape, 1)
    lane_f = lane.astype(F32)
    cur = jnp.where(lane < N_EXPERTS, logits, -jnp.inf)
    vals, idxs = [], []
    for _ in range(TOP_K):
        m = jnp.max(cur, axis=1, keepdims=True)
        ik = jnp.min(jnp.where(cur == m, lane_f, 128.0), axis=1, keepdims=True).astype(jnp.int32)
        vals.append(m)
        idxs.append(ik)
        cur = jnp.where(lane == ik, -jnp.inf, cur)
    es = [jnp.exp(v - vals[0]) for v in vals]
    denom = functools.reduce(lambda a, b: a + b, es)
    idx_out = jnp.zeros(logits.shape, F32)
    gate_out = jnp.zeros(logits.shape, F32)
    for kk in range(TOP_K):
        idx_out = jnp.where(lane == kk, idxs[kk].astype(F32), idx_out)
        gate_out = jnp.where(lane == kk, es[kk] / denom, gate_out)
    idt_ref[...] = jnp.transpose(idx_out)[0:8, :].astype(jnp.int32)
    gate_ref[...] = gate_out

    @pl.when(pl.program_id(0) == 0)
    def _():
        cnt_ref[...] = jnp.zeros(cnt_ref.shape, F32)

    hits = functools.reduce(lambda a, b: a + b, [jnp.where(lane == ik, 1.0, 0.0) for ik in idxs])
    cnt_ref[...] += jnp.sum(hits, axis=0, keepdims=True)


def _router(x, modt, norm_g, rw_hi, rw_lo, rb):
    tile = lambda i: (i, 0)
    const = lambda i: (0, 0)
    return pl.pallas_call(
        _router_kernel,
        out_shape=(jax.ShapeDtypeStruct((N_TOK * PACK_CHUNKS, 128), jnp.uint32),
                   jax.ShapeDtypeStruct((8, N_TOK), jnp.int32),
                   jax.ShapeDtypeStruct((N_TOK, 128), F32),
                   jax.ShapeDtypeStruct((1, 128), F32)),
        grid=(N_TILES,),
        in_specs=[pl.BlockSpec((TM, D), tile), pl.BlockSpec((1, 6, D), lambda i: (i, 0, 0)),
                  pl.BlockSpec((1, D), const), pl.BlockSpec((D, 128), const), pl.BlockSpec((D, 128), const),
                  pl.BlockSpec((1, 128), const)],
        out_specs=(pl.BlockSpec((TM * PACK_CHUNKS, 128), tile), pl.BlockSpec((8, TM), lambda i: (0, i)),
                   pl.BlockSpec((TM, 128), tile), pl.BlockSpec((1, 128), const)),
        compiler_params=_cp(1, VMEM_LIMIT),
    )(x, modt, norm_g, rw_hi, rw_lo, rb)


def _route_plan(idt, cnt):
    counts = cnt[0, 0:N_EXPERTS].astype(jnp.int32)
    nblk = (counts + MOE_BM - 1) // MOE_BM
    total = jnp.sum(nblk)
    pad = nblk * MOE_BM - counts
    experts = jnp.arange(N_EXPERTS, dtype=jnp.int32)
    filler = jnp.where(jnp.arange(MOE_BM - 1, dtype=jnp.int32)[None, :] < pad[:, None], experts[:, None], N_EXPERTS)
    n_tail = MOE_STEPS * MOE_BM - N_ASSIGN - N_EXPERTS * (MOE_BM - 1)
    keys = jnp.concatenate([idt[0:TOP_K].reshape(-1), filler.reshape(-1), jnp.full((n_tail,), N_EXPERTS, jnp.int32)])
    packed = lax.sort(keys * 65536 + jnp.arange(keys.shape[0], dtype=jnp.int32))
    order = packed & 65535
    b = jnp.arange(MOE_STEPS, dtype=jnp.int32)
    be = packed[::MOE_BM] >> 16
    be = jnp.where(b < total, be, be[jnp.maximum(total - 1, 0)])
    first = (((b == 0) | (be != jnp.concatenate([be[:1], be[:-1]]))) & (b < total)).astype(jnp.int32)
    return be, first, total.reshape(1), order


def _expert_kernel(be_ref, first_ref, total_ref, order_ref,
                   x_hbm, w1_ref, b1_ref, w2_ref, b2_ref, y4_hbm,
                   xs_buf, y_buf, y_acc, xs_bf, w1b, w2b, gsem, ssem):
    b = pl.program_id(0)
    total = total_ref[0]

    def tile_rows(r, per_row=LANE_CHUNKS):
        return pl.ds(pl.multiple_of(r * per_row, per_row), per_row)

    def slot_rows(slot, per_row=LANE_CHUNKS):
        return pl.ds(pl.multiple_of(slot * (MOE_BM * per_row), MOE_BM * per_row), MOE_BM * per_row)

    def gather_start(pos0, slot, row0, i):
        a = order_ref[pos0 + row0 + i]
        pltpu.make_async_copy(x_hbm.at[tile_rows(a & (N_TOK - 1), PACK_CHUNKS), :],
                              xs_buf.at[tile_rows(slot * MOE_BM + row0 + i, PACK_CHUNKS), :], gsem.at[slot]).start()

    def scatter_start(pos0, slot, row0, i):
        row = row0 + i
        a = order_ref[pos0 + row]
        dst = jnp.where(a < N_ASSIGN, a, N_ASSIGN + slot * MOE_BM + row)
        pltpu.make_async_copy(y_buf.at[tile_rows(slot * MOE_BM + row), :], y4_hbm.at[tile_rows(dst), :],
                              ssem.at[slot]).start(priority=1)

    def wait_gather(slot):
        pltpu.make_async_copy(x_hbm.at[slot_rows(0, PACK_CHUNKS), :], xs_buf.at[slot_rows(slot, PACK_CHUNKS), :],
                              gsem.at[slot]).wait()

    def wait_scatter(slot):
        pltpu.make_async_copy(y_buf.at[slot_rows(slot), :], y4_hbm.at[slot_rows(0), :], ssem.at[slot]).wait()

    def row_loop(fn):
        def body(g, c):
            for i in range(8):
                fn(g * 8, i)
            return c
        lax.fori_loop(0, MOE_BM // 8, body, 0)

    @pl.when(b == 0)
    def _():
        y_buf[...] = jnp.zeros(y_buf.shape, F32)
        for s in range(2):
            pltpu.make_async_copy(y_buf.at[slot_rows(s), :], y4_hbm.at[slot_rows(N_ASSIGN // MOE_BM + s), :],
                                  ssem.at[s]).start()
        row_loop(functools.partial(gather_start, 0, 0))
        row_loop(functools.partial(gather_start, jnp.minimum(1, total - 1) * MOE_BM, 1))

    @pl.when(b < total)
    def _():
        slot = b % MOE_SLOTS

        @pl.when(first_ref[b] == 1)
        def _():
            for j in range(2 * MOE_NHC):
                w1b[j] = w1_ref[0, 0, :, j * MOE_HC:(j + 1) * MOE_HC].astype(BF16)
            for j in range(MOE_NHC):
                w2b[j] = w2_ref[0, 0, j * MOE_HC:(j + 1) * MOE_HC, :].astype(BF16)

        wait_gather(slot)
        wait_scatter(slot)
        _load_packed_rows(xs_buf, slot * MOE_BM, MOE_BM, xs_bf)
        y_acc[...] = jnp.broadcast_to(b2_ref[0, 0], (MOE_BM, D))

        npos = jnp.minimum(b + 2, total - 1) * MOE_BM
        ppos = jnp.where(b >= 1, b - 1, MOE_STEPS - 1) * MOE_BM
        oslot = (b + 2) % MOE_SLOTS

        def hidden_chunk(j, c):
            xb = xs_bf[...]
            hg = _dot(xb, w1b[j]) + b1_ref[0, 0, pl.ds(j, 1), :]
            hu = _dot(xb, w1b[MOE_NHC + j]) + b1_ref[0, 0, pl.ds(MOE_NHC + j, 1), :]
            gate = jnp.minimum(hg, SWIGLU_LIMIT)
            up = jnp.clip(hu, -SWIGLU_LIMIT, SWIGLU_LIMIT)
            act = ((up + 1.0) * (gate * _sigmoid(SWIGLU_ALPHA * gate))).astype(BF16)
            y_acc[...] += _dot(act, w2b[j])
            for i in range(MOE_RPC):
                gather_start(npos, oslot, j * MOE_RPC, i)
                scatter_start(ppos, oslot, j * MOE_RPC, i)
            return c
        lax.fori_loop(0, MOE_NHC, hidden_chunk, 0)
        _store_token_tiles(y_buf, slot * MOE_BM, y_acc[...])

    @pl.when(b == total)
    def _():
        last = total - 1
        row_loop(functools.partial(scatter_start, last * MOE_BM, last % MOE_SLOTS))

    @pl.when(b == MOE_STEPS - 1)
    def _():
        for s in range(MOE_SLOTS):
            wait_scatter(s)
        wait_gather(total % MOE_SLOTS)
        wait_gather((total + 1) % MOE_SLOTS)


def _experts(layer, plan, h2, w1, b1, w2, b2):
    be, first, total, order = plan
    wmap = lambda b, be, *_: (layer, be[b], 0, 0)
    grid_spec = pltpu.PrefetchScalarGridSpec(
        num_scalar_prefetch=4,
        grid=(MOE_STEPS,),
        in_specs=[pl.BlockSpec(memory_space=pl.ANY),
                  pl.BlockSpec((1, 1, D, 2 * D), wmap), pl.BlockSpec((1, 1, 2 * MOE_NHC, MOE_HC), wmap),
                  pl.BlockSpec((1, 1, D, D), wmap), pl.BlockSpec((1, 1, 1, D), wmap)],
        out_specs=pl.BlockSpec(memory_space=pl.ANY),
        scratch_shapes=[pltpu.VMEM((MOE_SLOTS * MOE_BM * PACK_CHUNKS, 128), jnp.uint32),
                        pltpu.VMEM((MOE_SLOTS * MOE_BM * LANE_CHUNKS, 128), F32),
                        pltpu.VMEM((MOE_BM, D), F32), pltpu.VMEM((MOE_BM, D), BF16),
                        pltpu.VMEM((2 * MOE_NHC, D, MOE_HC), BF16), pltpu.VMEM((MOE_NHC, MOE_HC, D), BF16),
                        pltpu.SemaphoreType.DMA((MOE_SLOTS,)), pltpu.SemaphoreType.DMA((MOE_SLOTS,))],
    )
    return pl.pallas_call(
        _expert_kernel,
        out_shape=jax.ShapeDtypeStruct((Y4_ROWS * LANE_CHUNKS, 128), F32),
        grid_spec=grid_spec,
        compiler_params=_cp(1, VMEM_LIMIT, disable_bounds_checks=True),
    )(be, first, total, order, h2, w1, b1.reshape(DEPTH, N_EXPERTS, 2 * MOE_NHC, MOE_HC), w2,
      b2.reshape(DEPTH, N_EXPERTS, 1, D))


def _combine_kernel(final, x_ref, y0, y1, y2, y3, gate_ref, mod_ref, fg_ref, out_ref):
    gates = gate_ref[...]
    parts = []
    for c in range(LANE_CHUNKS):
        moe = gates[:, 0:1] * _load_token_tile_chunk(y0, 0, TM, c)
        for kk, y in ((1, y1), (2, y2), (3, y3)):
            moe = moe + gates[:, kk:kk + 1] * _load_token_tile_chunk(y, 0, TM, c)
        parts.append(moe)
    x = x_ref[...] + mod_ref[0, 5:6, :] * jnp.concatenate(parts, axis=1)
    out_ref[...] = _rms(x) * fg_ref[...] if final else x


def _combine(x, y4, gate, modt, final_g, final):
    tile = lambda i: (i, 0)
    plane = lambda kk: pl.BlockSpec((TM * LANE_CHUNKS, 128), lambda i: (kk * N_TILES + i, 0))
    return pl.pallas_call(
        functools.partial(_combine_kernel, final),
        out_shape=jax.ShapeDtypeStruct((N_TOK, D), F32),
        grid=(N_TILES,),
        in_specs=[pl.BlockSpec((TM, D), tile), plane(0), plane(1), plane(2), plane(3),
                  pl.BlockSpec((TM, 128), tile), pl.BlockSpec((1, 6, D), lambda i: (i, 0, 0)),
                  pl.BlockSpec((1, D), lambda i: (0, 0))],
        out_specs=pl.BlockSpec((TM, D), tile),
        compiler_params=_cp(1, VMEM_LIMIT),
    )(x, y4, y4, y4, y4, gate, modt, final_g)


def _moe(layer, x, modt, norm2_g, router_w, router_b, w1, b1, w2, b2, final_g, final):
    rw = jnp.pad(router_w, ((0, 0), (0, 128 - N_EXPERTS)))
    rw_hi, rw_lo = _split2(rw)
    rb = jnp.pad(router_b, (0, 128 - N_EXPERTS)).reshape(1, 128)
    h2, idt, gate, cnt = _router(x, modt, norm2_g, rw_hi, rw_lo, rb)
    y4 = _experts(layer, _route_plan(idt, cnt), h2, w1, b1, w2, b2)
    return _combine(x, y4, gate, modt, final_g, final)


def _rope_tables():
    def tables(n_tokens):
        rows = n_tokens // GRID_W
        row = jnp.repeat(jnp.arange(rows, dtype=F32), GRID_W)
        col = jnp.tile(jnp.arange(GRID_W, dtype=F32), rows)
        n_freq = QK_ROPE // 4
        inv = ROPE_THETA ** (-jnp.arange(n_freq, dtype=F32) / n_freq)
        ang = jnp.stack([row[:, None] * inv, col[:, None] * inv], axis=1)
        return jnp.cos(ang), jnp.sin(ang)

    cos, sin = tables(LAT_T)
    cos64 = jnp.concatenate([cos[:, 0], cos[:, 0], cos[:, 1], cos[:, 1]], axis=1)
    sin64 = jnp.concatenate([-sin[:, 0], sin[:, 0], -sin[:, 1], sin[:, 1]], axis=1)
    cos_all = jnp.concatenate([jnp.ones((N_CTX, QK_ROPE), F32)] + [cos64] * N_LAT_SEQ, axis=0)
    sin_all = jnp.concatenate([jnp.zeros((N_CTX, QK_ROPE), F32)] + [sin64] * N_LAT_SEQ, axis=0)
    return jnp.tile(cos_all, (1, 2)), jnp.tile(sin_all, (1, 2))


_PAIR_SWAP = np.concatenate([np.arange(16, 32), np.arange(0, 16), np.arange(48, 64), np.arange(32, 48)])


def _mla_weights(w_in, w_uq, w_ukv):
    kpe = w_in[:, Q_LORA + KV_LORA:]
    pad = jnp.zeros((D, 64), F32)
    w_in2 = jnp.concatenate([w_in[:, :Q_LORA + KV_LORA], kpe, pad, kpe[:, _PAIR_SWAP], pad], axis=1)
    uq = w_uq.reshape(Q_LORA, MLA_HEADS, QK_NOPE + QK_ROPE)
    rope = uq[:, :, QK_NOPE:]
    zpad = jnp.zeros((Q_LORA, MLA_HEADS, 128 - QK_ROPE), F32)
    w_uq2 = jnp.concatenate([uq[:, :, :QK_NOPE].reshape(Q_LORA, NOPE_ALL),
                             jnp.concatenate([rope, zpad], axis=2).reshape(Q_LORA, ROPE_PAD_ALL),
                             jnp.concatenate([rope[:, :, _PAIR_SWAP], zpad], axis=2).reshape(Q_LORA, ROPE_PAD_ALL)],
                            axis=1)
    ukv = w_ukv.reshape(KV_LORA, MLA_HEADS, QK_NOPE + V_HEAD)
    w_ukv2 = jnp.concatenate([ukv[:, :, :QK_NOPE].reshape(KV_LORA, NOPE_ALL),
                              ukv[:, :, QK_NOPE:].reshape(KV_LORA, NOPE_ALL)], axis=1)
    return w_in2.astype(BF16), w_uq2.astype(BF16), w_ukv2.astype(BF16)


def _ml_init_state(state_c, state_n, state_m, j, direction):
    c = state_c[:, j, direction].astype(F32)
    n = state_n[:, j, direction].astype(F32)[..., None]
    m = jnp.broadcast_to(state_m[:, j, direction].astype(F32)[..., None, None], n.shape)
    pad = jnp.zeros(c.shape[:-1] + (ML_AUG - ML_DV - 2,), F32)
    lat = jnp.concatenate([c, n, m, pad], axis=-1)
    return jnp.concatenate([jnp.zeros((N_CTX_SEQ,) + lat.shape[1:], F32), lat], axis=0)


def kernel(x_prompt, x_sample, cache_mla_ckv, cache_mla_kpe, state_mlstm_C, state_mlstm_n, state_mlstm_m, c, c_ctx, norm1_g, norm2_g, ada_w, ada_b, ml_w_in, ml_gate_b, ml_norm_g, ml_w_out, mla_w_in, mla_q_norm_g, mla_w_uq, mla_kv_norm_g, mla_w_ukv, mla_w_out, fn_w_out, router_w, router_b, exp_w1, exp_b1, exp_w2, exp_b2, final_g):
    x = jnp.concatenate([x_prompt.reshape(N_CTX, D), x_sample.reshape(N_LAT_SEQ * LAT_T, D)], axis=0)

    cond8 = jnp.concatenate([c_ctx[None, :], c, jnp.zeros((8 - 1 - N_LAT_SEQ, D), F32)], axis=0)
    mod = _modulation(cond8, ada_w, ada_b)
    tile_cond = np.concatenate([np.zeros(CTX_TILES, np.int32)] +
                               [np.full(LAT_TILES, 1 + s, np.int32) for s in range(N_LAT_SEQ)])
    modt = mod[:, tile_cond].reshape(DEPTH, N_TILES, 6, D)

    hk = ML_HEADS * ML_DK
    states = []
    new_ckv = new_kpe = None
    for l in range(DEPTH):
        kind, j = l % 3, l // 3
        n1 = norm1_g[l].reshape(1, D)
        if kind == 0:
            w = ml_w_in[j]
            w_gates = w[:, 2 * hk + 2 * D:]
            qkv, o, gg, gt = _ml_in(x, modt[l], n1, w[:, :2 * hk + 2 * D].astype(BF16),
                                    jnp.pad(w_gates, ((0, 0), (0, 128 - 4 * ML_HEADS))).astype(BF16),
                                    w_gates.T.astype(BF16))
            gb = ml_gate_b[j].reshape(4 * ML_HEADS).astype(F32)
            gb_row = jnp.pad(gb, (0, 128 - 4 * ML_HEADS)).reshape(1, 128)
            gb_col = jnp.broadcast_to(gb[:, None], (4 * ML_HEADS, 128))
            hf, hb, st_f, st_b = _ml_core(qkv, gg, gt, gb_row, gb_col,
                                          _ml_init_state(state_mlstm_C, state_mlstm_n, state_mlstm_m, j, 0),
                                          _ml_init_state(state_mlstm_C, state_mlstm_n, state_mlstm_m, j, 1))
            states.append((st_f[:N_CTX_SEQ], st_b[:N_CTX_SEQ]))
            x = _ml_out(hf, hb, o, ml_norm_g[j].reshape(1, D), ml_w_out[j].astype(BF16), x, modt[l])
        elif kind == 1:
            w_in2, w_uq2, w_ukv2 = _mla_weights(mla_w_in[j], mla_w_uq[j], mla_w_ukv[j])
            cos2, sin2 = _rope_tables()
            q, ckv, kpe = _mla_in(x, modt[l], n1, w_in2, mla_q_norm_g[j].reshape(1, Q_LORA),
                                  mla_kv_norm_g[j].reshape(1, KV_LORA), w_uq2, cos2, sin2)
            new_ckv, new_kpe = ckv[:N_CTX], kpe[:N_CTX, 0:QK_ROPE]
            k_tok, v_tok = _kv_expand(ckv, kpe, w_ukv2)
            kr_past = jnp.pad(cache_mla_kpe[:, j].reshape(N_LAT_SEQ * PAST_LEN, QK_ROPE), ((0, 0), (0, 128 - QK_ROPE)))
            k_past, v_past = _kv_expand(cache_mla_ckv[:, j].reshape(N_LAT_SEQ * PAST_LEN, KV_LORA), kr_past, w_ukv2)
            att = _attention(q, k_tok, v_tok, k_past, v_past)
            x = _res_linear(att, mla_w_out[j].astype(BF16), x, modt[l])
        else:
            x = _fourier(x, modt[l], n1, _dft_tables(), fn_w_out[j].astype(BF16))
        x = _moe(l, x, modt[l], norm2_g[l].reshape(1, D), router_w[l], router_b[l],
                 exp_w1, exp_b1, exp_w2, exp_b2, final_g.reshape(1, D), l == DEPTH - 1)

    y_prompt = x[:N_CTX].reshape(N_CTX_SEQ, CTX_T, D)
    y_sample = x[N_CTX:].reshape(N_LAT_SEQ, LAT_T, D)
    new_mla_ckv = new_ckv.reshape(N_CTX_SEQ, 1, CTX_T, KV_LORA)
    new_mla_kpe = new_kpe.reshape(N_CTX_SEQ, 1, CTX_T, QK_ROPE)
    st = jnp.stack([jnp.stack([sf, sb], axis=1) for sf, sb in states], axis=1)
    new_c = st[..., 0:ML_DV]
    new_n = st[..., ML_N_LANE]
    new_m = st[..., 0, ML_M_LANE]
    return (y_prompt, y_sample, new_mla_ckv, new_mla_kpe, new_c, new_n, new_m)
```

```python
import functools

import numpy as np
import jax
import jax.numpy as jnp
from jax import lax
from jax.experimental import pallas as pl
from jax.experimental.pallas import tpu as pltpu

F32 = jnp.float32
BF16 = jnp.bfloat16

D = 1024
DEPTH = 4
N_CTX_SEQ, CTX_T = 16, 256
N_LAT_SEQ, LAT_T = 2, 2048
N_CTX = N_CTX_SEQ * CTX_T
N_TOK = N_CTX + N_LAT_SEQ * LAT_T
TM = 256
N_TILES = N_TOK // TM
CTX_TILES = N_CTX // TM
LAT_TILES = LAT_T // TM
N_SEQ = N_CTX_SEQ + N_LAT_SEQ
PAST_LEN = 512
GRID_W = 64
RMS_EPS = 1e-6

ML_HEADS, ML_DK, ML_DV = 8, 64, 128
ML_AUG = 256
ML_N_LANE, ML_M_LANE = ML_DV, ML_DV + 1

MLA_HEADS, QK_NOPE, QK_ROPE, V_HEAD = 16, 128, 64, 128
Q_LORA, KV_LORA = 384, 256
ROPE_THETA = 10000.0
ATT_HG = 4
ATT_SCALE = (QK_NOPE + QK_ROPE) ** -0.5

FN_GROUPS, FN_GW = 4, 256

N_EXPERTS, TOP_K = 32, 4
SWIGLU_ALPHA, SWIGLU_LIMIT = 1.702, 7.0
MOE_BM = 256
N_ASSIGN = N_TOK * TOP_K
MOE_NB = N_ASSIGN // MOE_BM + N_EXPERTS
MOE_STEPS = MOE_NB + 1
MOE_SLOTS = 3
MOE_HC = 256
MOE_NHC = D // MOE_HC
MOE_RPC = MOE_BM // MOE_NHC
Y4_ROWS = N_ASSIGN + MOE_SLOTS * MOE_BM

VMEM_LIMIT = 56 * 1024 * 1024


def _cp(n_grid_axes, vmem=None, **kw):
    return pltpu.CompilerParams(dimension_semantics=("arbitrary",) * n_grid_axes, vmem_limit_bytes=vmem, **kw)


def _dot(a, b):
    return jnp.dot(a, b, preferred_element_type=F32)


def _dot_nt(a, b):
    return lax.dot_general(a, b, (((1,), (1,)), ((), ())), preferred_element_type=F32)


def _dot_tn(a, b):
    return lax.dot_general(a, b, (((0,), (0,)), ((), ())), preferred_element_type=F32)


def _sigmoid(x):
    return 1.0 / (1.0 + jnp.exp(-x))


def _log_sigmoid(x):
    return jnp.minimum(x, 0.0) - jnp.log(1.0 + jnp.exp(-jnp.abs(x)))


def _rms(x):
    return x * lax.rsqrt(jnp.mean(x * x, axis=-1, keepdims=True) + RMS_EPS)


def _norm_mod(x, g, shift, scale):
    return (_rms(x) * g) * (1.0 + scale) + shift


def _split2(x):
    hi = x.astype(BF16)
    lo = (x - hi.astype(F32)).astype(BF16)
    return hi, lo


def _split3(x):
    hi = x.astype(BF16)
    r = x - hi.astype(F32)
    mid = r.astype(BF16)
    lo = (r - mid.astype(F32)).astype(BF16)
    return hi, mid, lo


LANE_CHUNKS = D // 128


def _store_token_tiles(ref, row0, val):
    n = val.shape[0]
    for c in range(LANE_CHUNKS):
        ref[pl.ds(row0 * LANE_CHUNKS + c, n, stride=LANE_CHUNKS), :] = val[:, c * 128:(c + 1) * 128]


def _load_token_tile_chunk(ref, row0, n, c):
    return ref[pl.ds(row0 * LANE_CHUNKS + c, n, stride=LANE_CHUNKS), :]


PACK_CHUNKS = LANE_CHUNKS // 2
HI16 = 0xFFFF0000


def _store_packed_rows(ref, val):
    n = val.shape[0]
    for c in range(PACK_CHUNKS):
        lo = pltpu.bitcast(val[:, c * 128:(c + 1) * 128].astype(BF16).astype(F32), jnp.uint32)
        hi = pltpu.bitcast(val[:, D // 2 + c * 128:D // 2 + (c + 1) * 128].astype(BF16).astype(F32), jnp.uint32)
        ref[pl.ds(c, n, stride=PACK_CHUNKS), :] = (hi & jnp.uint32(HI16)) | (lo >> jnp.uint32(16))


def _load_packed_rows(ref, row0, n, out_ref):
    for c in range(PACK_CHUNKS):
        u = ref[pl.ds(row0 * PACK_CHUNKS + c, n, stride=PACK_CHUNKS), :]
        out_ref[:, c * 128:(c + 1) * 128] = pltpu.bitcast(u << jnp.uint32(16), F32).astype(BF16)
        out_ref[:, D // 2 + c * 128:D // 2 + (c + 1) * 128] = pltpu.bitcast(u & jnp.uint32(HI16), F32).astype(BF16)


def _tile_seq(r):
    return jnp.where(r < CTX_TILES, r, CTX_TILES + (r - CTX_TILES) // LAT_TILES)


def _lat_seq(r):
    return jnp.clip((r - CTX_TILES) // LAT_TILES, 0, N_LAT_SEQ - 1)


MOD_TN = 1536


def _mod_kernel(c_ref, w_ref, b_ref, o_ref):
    a = c_ref[...]
    s_hi, s_lo = _split2(a * _sigmoid(a))
    w_hi, w_lo = _split2(w_ref[0])
    o_ref[0] = _dot_split(s_hi, s_lo, w_hi, w_lo) + b_ref[0]


def _modulation(cond8, ada_w, ada_b):
    n_col = ada_w.shape[-1]
    return pl.pallas_call(
        _mod_kernel,
        out_shape=jax.ShapeDtypeStruct((DEPTH, 8, n_col), F32),
        grid=(DEPTH, n_col // MOD_TN),
        in_specs=[pl.BlockSpec((8, D), lambda l, j: (0, 0)),
                  pl.BlockSpec((1, D, MOD_TN), lambda l, j: (l, 0, j)),
                  pl.BlockSpec((1, 1, MOD_TN), lambda l, j: (l, 0, j))],
        out_specs=pl.BlockSpec((1, 8, MOD_TN), lambda l, j: (l, 0, j)),
        compiler_params=_cp(2, VMEM_LIMIT),
    )(cond8, ada_w, ada_b.reshape(DEPTH, 1, n_col))


def _ml_in_kernel(x_ref, mod_ref, g_ref, w_ref, wg_ref, wgt_ref, qkv_ref, o_ref, gg_ref, gt_ref):
    h = _norm_mod(x_ref[...], g_ref[...], mod_ref[0, 0:1, :], mod_ref[0, 1:2, :]).astype(BF16)
    hk = ML_HEADS * ML_DK
    lane = lax.broadcasted_iota(jnp.int32, (1, 2 * hk), 1)
    qscale = jnp.where(lane < hk, ML_DK ** -0.5, 1.0).astype(F32)
    qkv_ref[:, 0:2 * hk] = (_dot(h, w_ref[:, 0:2 * hk]) * qscale).astype(BF16)
    qkv_ref[:, 2 * hk:2 * hk + D] = _dot(h, w_ref[:, 2 * hk:2 * hk + D]).astype(BF16)
    o_ref[...] = _dot(h, w_ref[:, 2 * hk + D:2 * hk + 2 * D])
    gg_ref[...] = _dot(h, wg_ref[...])
    gt_ref[...] = _dot_nt(wgt_ref[...], h)


def _ml_in(x, modt, norm_g, w_main, w_g, w_gt):
    tile = lambda i: (i, 0)
    const = lambda i: (0, 0)
    return pl.pallas_call(
        _ml_in_kernel,
        out_shape=(jax.ShapeDtypeStruct((N_TOK, 2 * D), BF16),
                   jax.ShapeDtypeStruct((N_TOK, D), F32),
                   jax.ShapeDtypeStruct((N_TOK, 128), F32),
                   jax.ShapeDtypeStruct((32, N_TOK), F32)),
        grid=(N_TILES,),
        in_specs=[pl.BlockSpec((TM, D), tile),
                  pl.BlockSpec((1, 6, D), lambda i: (i, 0, 0)),
                  pl.BlockSpec((1, D), const),
                  pl.BlockSpec((D, 3 * D), const),
                  pl.BlockSpec((D, 128), const),
                  pl.BlockSpec((32, D), const)],
        out_specs=(pl.BlockSpec((TM, 2 * D), tile),
                   pl.BlockSpec((TM, D), tile),
                   pl.BlockSpec((TM, 128), tile),
                   pl.BlockSpec((32, TM), lambda i: (0, i))),
        compiler_params=_cp(1, VMEM_LIMIT),
    )(x, modt, norm_g, w_main, w_g, w_gt)


def _ml_direction(lower, q_ref, k_ref, v_ref, g_ref, gt_ref, gbrow_ref, gbcol_ref, h_ref, st_ref, c_s, m_s):
    L = TM
    d_off = 0 if lower else ML_HEADS
    row = lax.broadcasted_iota(jnp.int32, (L, L), 0)
    col = lax.broadcasted_iota(jnp.int32, (L, L), 1)
    mask = (col <= row) if lower else (col >= row)
    tri = jnp.where(mask, 1.0, 0.0).astype(BF16)

    lane = lax.broadcasted_iota(jnp.int32, (L, 128), 1)
    gc = g_ref[...] + gbrow_ref[...]
    gc = jnp.where((lane >= 2 * ML_HEADS) & (lane < 4 * ML_HEADS), _log_sigmoid(gc), gc)
    bc = sum(_dot(tri, p) for p in _split3(gc))
    sub = lax.broadcasted_iota(jnp.int32, (32, L), 0)
    gr = gt_ref[...] + gbcol_ref[:, 0:1]
    gr = jnp.where(sub >= 2 * ML_HEADS, _log_sigmoid(gr), gr)
    br = sum(_dot_nt(p, tri) for p in _split3(gr))

    q = q_ref[...]
    k = k_ref[...]
    v = v_ref[...]
    lane_a = lax.broadcasted_iota(jnp.int32, (L, ML_AUG - ML_DV), 1)
    ones_blk = jnp.where(lane_a == 0, 1.0, 0.0).astype(BF16)
    lane_s = lax.broadcasted_iota(jnp.int32, (ML_DK, ML_AUG), 1)

    for hd in range(ML_HEADS):
        ci = d_off + hd
        cf = 2 * ML_HEADS + d_off + hd
        i_col, b_col = gc[:, ci:ci + 1], bc[:, cf:cf + 1]
        i_row, b_row = gr[ci:ci + 1, :], br[cf:cf + 1, :]
        total = b_col[L - 1:L, :] if lower else b_col[0:1, :]
        m = m_s[hd][0:1, 0:1]
        c_aug = c_s[hd]
        rmat = jnp.where(mask, i_row - b_row, -jnp.inf)
        big_m = jnp.max(rmat, axis=1, keepdims=True)
        mm = jnp.maximum(m, big_m)
        qh = q[:, hd * ML_DK:(hd + 1) * ML_DK]
        kh = k[:, hd * ML_DK:(hd + 1) * ML_DK]
        v_aug = jnp.concatenate([v[:, hd * ML_DV:(hd + 1) * ML_DV], ones_blk], axis=1)
        p = (_dot_nt(qh, kh) * jnp.exp(rmat - big_m)).astype(BF16)
        num = jnp.exp(big_m - mm) * _dot(p, v_aug) + jnp.exp(m - mm) * _dot(qh, c_aug.astype(BF16))
        den = jnp.maximum(jnp.abs(num[:, ML_N_LANE:ML_N_LANE + 1]), jnp.exp(-(b_col + mm)))
        h_ref[:, hd * ML_DV:(hd + 1) * ML_DV] = num[:, 0:ML_DV] / den

        g_col = (total - b_col) + i_col
        m_new = jnp.maximum(total + m, jnp.max(g_col, axis=0, keepdims=True))
        kw = (kh.astype(F32) * jnp.exp(g_col - m_new)).astype(BF16)
        c_new = jnp.exp(total + m - m_new) * c_aug + _dot_tn(kw, v_aug)
        c_s[hd] = c_new
        m_s[hd] = jnp.broadcast_to(m_new, (8, 128))
        st_ref[0, hd] = jnp.where(lane_s == ML_M_LANE, m_new, c_new)


def _ml_core_kernel(qf, kf, vf, gf, gtf, qb, kb, vb, gb, gtb, gbrow, gbcol, initf, initb,
                    hf_ref, hb_ref, stf_ref, stb_ref, cf_s, cb_s, mf_s, mb_s):
    i = pl.program_id(0)
    start_f = (i <= CTX_TILES) | (i == CTX_TILES + LAT_TILES)
    start_b = (i == 0) | (i == LAT_TILES) | (i >= 2 * LAT_TILES)

    def load_state(init_ref, c_s, m_s):
        c_s[...] = init_ref[0]
        for hd in range(ML_HEADS):
            m_s[hd] = jnp.broadcast_to(init_ref[0, hd][0:1, ML_M_LANE:ML_M_LANE + 1], (8, 128))

    @pl.when(start_f)
    def _():
        load_state(initf, cf_s, mf_s)

    @pl.when(start_b)
    def _():
        load_state(initb, cb_s, mb_s)

    _ml_direction(True, qf, kf, vf, gf, gtf, gbrow, gbcol, hf_ref, stf_ref, cf_s, mf_s)
    _ml_direction(False, qb, kb, vb, gb, gtb, gbrow, gbcol, hb_ref, stb_ref, cb_s, mb_s)


def _ml_core(qkv, gg, gt, gb_row, gb_col, init_f, init_b):
    last = N_TILES - 1
    fwd = lambda i: i
    bwd = lambda i: last - i
    hk = ML_HEADS * ML_DK

    def specs(t):
        return [pl.BlockSpec((TM, hk), lambda i: (t(i), 0)),
                pl.BlockSpec((TM, hk), lambda i: (t(i), 1)),
                pl.BlockSpec((TM, D), lambda i: (t(i), 1)),
                pl.BlockSpec((TM, 128), lambda i: (t(i), 0)),
                pl.BlockSpec((32, TM), lambda i: (0, t(i)))]

    def st_spec(t):
        return pl.BlockSpec((1, ML_HEADS, ML_DK, ML_AUG), lambda i: (_tile_seq(t(i)), 0, 0, 0))

    st_shape = jax.ShapeDtypeStruct((N_SEQ, ML_HEADS, ML_DK, ML_AUG), F32)
    return pl.pallas_call(
        _ml_core_kernel,
        out_shape=(jax.ShapeDtypeStruct((N_TOK, D), F32), jax.ShapeDtypeStruct((N_TOK, D), F32),
                   st_shape, st_shape),
        grid=(N_TILES,),
        in_specs=specs(fwd) + specs(bwd) + [
            pl.BlockSpec((1, 128), lambda i: (0, 0)),
            pl.BlockSpec((32, 128), lambda i: (0, 0)),
            st_spec(fwd), st_spec(bwd)],
        out_specs=(pl.BlockSpec((TM, D), lambda i: (i, 0)),
                   pl.BlockSpec((TM, D), lambda i: (last - i, 0)),
                   st_spec(fwd), st_spec(bwd)),
        scratch_shapes=[pltpu.VMEM((ML_HEADS, ML_DK, ML_AUG), F32),
                        pltpu.VMEM((ML_HEADS, ML_DK, ML_AUG), F32),
                        pltpu.VMEM((ML_HEADS, 8, 128), F32),
                        pltpu.VMEM((ML_HEADS, 8, 128), F32)],
        compiler_params=_cp(1, VMEM_LIMIT),
    )(qkv, qkv, qkv, gg, gt, qkv, qkv, qkv, gg, gt, gb_row, gb_col, init_f, init_b)


def _ml_out_kernel(hf_ref, hb_ref, o_ref, ng_ref, w_ref, x_ref, mod_ref, out_ref, z_s):
    hh = hf_ref[...] + hb_ref[...]
    o = o_ref[...]
    for hd in range(ML_HEADS):
        sl = slice(hd * ML_DV, (hd + 1) * ML_DV)
        z_s[:, sl] = (_sigmoid(o[:, sl]) * (_rms(hh[:, sl]) * ng_ref[:, sl])).astype(BF16)
    out_ref[...] = x_ref[...] + mod_ref[0, 2:3, :] * _dot(z_s[...], w_ref[...])


def _ml_out(hf, hb, o, norm_g, w_out, x, modt):
    tile = lambda i: (i, 0)
    const = lambda i: (0, 0)
    return pl.pallas_call(
        _ml_out_kernel,
        out_shape=jax.ShapeDtypeStruct((N_TOK, D), F32),
        grid=(N_TILES,),
        in_specs=[pl.BlockSpec((TM, D), tile), pl.BlockSpec((TM, D), tile), pl.BlockSpec((TM, D), tile),
                  pl.BlockSpec((1, D), const), pl.BlockSpec((D, D), const),
                  pl.BlockSpec((TM, D), tile), pl.BlockSpec((1, 6, D), lambda i: (i, 0, 0))],
        out_specs=pl.BlockSpec((TM, D), tile),
        scratch_shapes=[pltpu.VMEM((TM, D), BF16)],
        compiler_params=_cp(1, VMEM_LIMIT),
    )(hf, hb, o, norm_g, w_out, x, modt)


MLA_IN_COLS = 896
NOPE_ALL = MLA_HEADS * QK_NOPE
ROPE_ALL = MLA_HEADS * QK_ROPE


QK_CAT = 256
QCAT_ALL = MLA_HEADS * QK_CAT
ROPE_PAD_ALL = MLA_HEADS * 128


def _mla_in_kernel(x_ref, mod_ref, g_ref, w_ref, qg_ref, kvg_ref, wuq_ref, cos_ref, sin_ref,
                   q_ref, ckv_ref, kpe_ref):
    h = _norm_mod(x_ref[...], g_ref[...], mod_ref[0, 0:1, :], mod_ref[0, 1:2, :]).astype(BF16)
    c = _dot(h, w_ref[...])
    cos, sin = cos_ref[...], sin_ref[...]
    ckv_ref[...] = _rms(c[:, Q_LORA:Q_LORA + KV_LORA]) * kvg_ref[...]
    kpe_ref[...] = c[:, 640:768] * cos + c[:, 768:896] * sin
    cq = (_rms(c[:, 0:Q_LORA]) * qg_ref[...]).astype(BF16)
    nope = _dot(cq, wuq_ref[:, 0:NOPE_ALL])
    rope = _dot(cq, wuq_ref[:, NOPE_ALL:NOPE_ALL + ROPE_PAD_ALL])
    swapped = _dot(cq, wuq_ref[:, NOPE_ALL + ROPE_PAD_ALL:NOPE_ALL + 2 * ROPE_PAD_ALL])
    for hd in range(MLA_HEADS):
        sl = slice(hd * 128, (hd + 1) * 128)
        q_ref[:, hd * QK_CAT:hd * QK_CAT + QK_NOPE] = nope[:, sl].astype(BF16)
        q_ref[:, hd * QK_CAT + QK_NOPE:(hd + 1) * QK_CAT] = (rope[:, sl] * cos + swapped[:, sl] * sin).astype(BF16)


def _mla_in(x, modt, norm_g, w_in2, qg, kvg, w_uq2, cos2, sin2):
    tile = lambda i: (i, 0)
    const = lambda i: (0, 0)
    return pl.pallas_call(
        _mla_in_kernel,
        out_shape=(jax.ShapeDtypeStruct((N_TOK, QCAT_ALL), BF16),
                   jax.ShapeDtypeStruct((N_TOK, KV_LORA), F32),
                   jax.ShapeDtypeStruct((N_TOK, 128), F32)),
        grid=(N_TILES,),
        in_specs=[pl.BlockSpec((TM, D), tile), pl.BlockSpec((1, 6, D), lambda i: (i, 0, 0)),
                  pl.BlockSpec((1, D), const), pl.BlockSpec((D, MLA_IN_COLS), const),
                  pl.BlockSpec((1, Q_LORA), const), pl.BlockSpec((1, KV_LORA), const),
                  pl.BlockSpec((Q_LORA, NOPE_ALL + 2 * ROPE_PAD_ALL), const),
                  pl.BlockSpec((TM, 128), tile), pl.BlockSpec((TM, 128), tile)],
        out_specs=(pl.BlockSpec((TM, QCAT_ALL), tile),
                   pl.BlockSpec((TM, KV_LORA), tile), pl.BlockSpec((TM, 128), tile)),
        compiler_params=_cp(1, VMEM_LIMIT),
    )(x, modt, norm_g, w_in2, qg, kvg, w_uq2, cos2, sin2)


def _kv_expand_kernel(c_ref, kr_ref, w_ref, k_ref, v_ref):
    c = c_ref[...].astype(BF16)
    kn = _dot(c, w_ref[:, 0:NOPE_ALL])
    kr = kr_ref[...].astype(BF16)
    for hd in range(MLA_HEADS):
        k_ref[:, hd * QK_CAT:hd * QK_CAT + QK_NOPE] = kn[:, hd * QK_NOPE:(hd + 1) * QK_NOPE].astype(BF16)
        k_ref[:, hd * QK_CAT + QK_NOPE:(hd + 1) * QK_CAT] = kr
    v_ref[...] = _dot(c, w_ref[:, NOPE_ALL:2 * NOPE_ALL]).astype(BF16)


def _kv_expand(ckv, kr, w_ukv2):
    rows = ckv.shape[0]
    tile = lambda i: (i, 0)
    return pl.pallas_call(
        _kv_expand_kernel,
        out_shape=(jax.ShapeDtypeStruct((rows, QCAT_ALL), BF16), jax.ShapeDtypeStruct((rows, NOPE_ALL), BF16)),
        grid=(rows // TM,),
        in_specs=[pl.BlockSpec((TM, KV_LORA), tile), pl.BlockSpec((TM, 128), tile),
                  pl.BlockSpec((KV_LORA, 2 * NOPE_ALL), lambda i: (0, 0))],
        out_specs=(pl.BlockSpec((TM, QCAT_ALL), tile), pl.BlockSpec((TM, NOPE_ALL), tile)),
        compiler_params=_cp(1, VMEM_LIMIT),
    )(ckv, kr, w_ukv2)


def _attend(q_ref, segs, out_ref):
    for hd in range(ATT_HG):
        ksl = slice(hd * QK_CAT, (hd + 1) * QK_CAT)
        vsl = slice(hd * V_HEAD, (hd + 1) * V_HEAD)
        q = q_ref[:, ksl]
        scores = [_dot_nt(q, k_ref[:, ksl]) * ATT_SCALE for k_ref, _ in segs]
        m = functools.reduce(jnp.maximum, [jnp.max(s, axis=1, keepdims=True) for s in scores])
        ps = [jnp.exp(s - m) for s in scores]
        denom = sum(jnp.sum(p, axis=1, keepdims=True) for p in ps)
        acc = sum(_dot(p.astype(BF16), v_ref[:, vsl]) for p, (_, v_ref) in zip(ps, segs))
        out_ref[:, vsl] = (acc / denom).astype(BF16)


def _attn_kernel(q_ref, kc_ref, vc_ref, kl_ref, vl_ref, kp_ref, vp_ref, out_ref):
    r = pl.program_id(1)

    @pl.when(r < CTX_TILES)
    def _():
        _attend(q_ref, [(kc_ref, vc_ref)], out_ref)

    @pl.when(r >= CTX_TILES)
    def _():
        _attend(q_ref, [(kp_ref, vp_ref), (kl_ref, vl_ref)], out_ref)


def _attention(q, k_tok, v_tok, k_past, v_past):
    kw, vw = ATT_HG * QK_CAT, ATT_HG * V_HEAD
    ctx_t = lambda g, r: (jnp.minimum(r, CTX_TILES - 1), g)
    lat_t = lambda g, r: (N_CTX // LAT_T + _lat_seq(r), g)
    past_t = lambda g, r: (_lat_seq(r), g)
    return pl.pallas_call(
        _attn_kernel,
        out_shape=jax.ShapeDtypeStruct((N_TOK, NOPE_ALL), BF16),
        grid=(MLA_HEADS // ATT_HG, N_TILES),
        in_specs=[pl.BlockSpec((TM, kw), lambda g, r: (r, g)),
                  pl.BlockSpec((CTX_T, kw), ctx_t), pl.BlockSpec((CTX_T, vw), ctx_t),
                  pl.BlockSpec((LAT_T, kw), lat_t), pl.BlockSpec((LAT_T, vw), lat_t),
                  pl.BlockSpec((PAST_LEN, kw), past_t), pl.BlockSpec((PAST_LEN, vw), past_t)],
        out_specs=pl.BlockSpec((TM, vw), lambda g, r: (r, g)),
        compiler_params=_cp(2, VMEM_LIMIT),
    )(q, k_tok, v_tok, k_tok, v_tok, k_past, v_past)


def _res_linear_kernel(a_ref, w_ref, x_ref, mod_ref, out_ref):
    out_ref[...] = x_ref[...] + mod_ref[0, 2:3, :] * _dot(a_ref[...], w_ref[...])


def _res_linear(a, w, x, modt):
    k = a.shape[1]
    tile = lambda i: (i, 0)
    return pl.pallas_call(
        _res_linear_kernel,
        out_shape=jax.ShapeDtypeStruct((N_TOK, D), F32),
        grid=(N_TILES,),
        in_specs=[pl.BlockSpec((TM, k), tile), pl.BlockSpec((k, D), lambda i: (0, 0)),
                  pl.BlockSpec((TM, D), tile), pl.BlockSpec((1, 6, D), lambda i: (i, 0, 0))],
        out_specs=pl.BlockSpec((TM, D), tile),
        compiler_params=_cp(1, VMEM_LIMIT),
    )(a, w, x, modt)


def _dot_split(a_hi, a_lo, b_hi, b_lo):
    return _dot(a_hi, b_hi) + (_dot(a_hi, b_lo) + _dot(a_lo, b_hi))


def _fn_channel_dft(x, g, shift, scale, wc_hi, wc_lo):
    h_hi, h_lo = _split2(_norm_mod(x, g, shift, scale))
    a_parts, b_parts = [], []
    for grp in range(FN_GROUPS):
        sl = slice(grp * FN_GW, (grp + 1) * FN_GW)
        ab = _dot_split(h_hi[:, sl], h_lo[:, sl], wc_hi, wc_lo)
        a_parts.append(ab[:, 0:FN_GW])
        b_parts.append(ab[:, FN_GW:2 * FN_GW])
    return jnp.concatenate(a_parts, axis=1), jnp.concatenate(b_parts, axis=1)


FN_STEPS = CTX_TILES + N_LAT_SEQ * 2 * LAT_TILES


def _fn_step(t):
    u = jnp.maximum(t - CTX_TILES, 0)
    seq, ph, tile = u // (2 * LAT_TILES), (u % (2 * LAT_TILES)) // LAT_TILES, u % LAT_TILES
    is_ctx = t < CTX_TILES
    return is_ctx, ph, tile, jnp.where(is_ctx, t, CTX_TILES + seq * LAT_TILES + tile)


def _fn_kernel(x_ref, mod_ref, g_ref, wc_ref, tc_ref, tl_ref, w_ref, out_ref, ab_hi_s, ab_lo_s):
    is_ctx, ph, tile, _ = _fn_step(pl.program_id(0))
    g = g_ref[...]
    shift, scale, gate = mod_ref[0, 0:1, :], mod_ref[0, 1:2, :], mod_ref[0, 2:3, :]
    wc_hi, wc_lo = _split2(wc_ref[...])

    def finish(f, t_len):
        f = f * ((t_len * FN_GW) ** -0.5)
        out_ref[...] = x_ref[...] + gate * _dot(f.astype(BF16), w_ref[...])

    @pl.when(is_ctx)
    def _():
        a, b = _fn_channel_dft(x_ref[...], g, shift, scale, wc_hi, wc_lo)
        ab_hi, ab_lo = _split2(jnp.concatenate([a, b], axis=0))
        finish(_dot_split(*_split2(tc_ref[...]), ab_hi, ab_lo), CTX_T)

    @pl.when(jnp.logical_not(is_ctx) & (ph == 0))
    def _():
        a, b = _fn_channel_dft(x_ref[...], g, shift, scale, wc_hi, wc_lo)
        a_hi, a_lo = _split2(a)
        b_hi, b_lo = _split2(b)
        rows = pl.ds(pl.multiple_of(tile * TM, TM), TM)
        rows_b = pl.ds(pl.multiple_of(LAT_T + tile * TM, TM), TM)
        ab_hi_s[rows, :] = a_hi
        ab_lo_s[rows, :] = a_lo
        ab_hi_s[rows_b, :] = b_hi
        ab_lo_s[rows_b, :] = b_lo

    @pl.when(jnp.logical_not(is_ctx) & (ph == 1))
    def _():
        finish(_dot_split(*_split2(tl_ref[...]), ab_hi_s[...], ab_lo_s[...]), LAT_T)


def _fourier(x, modt, norm_g, tabs, w_out):
    wc, tc, tl = tabs
    const = lambda t: (0, 0)
    x_tile = lambda t: (_fn_step(t)[3], 0)

    def out_tile(t):
        is_ctx, ph, tile, gt = _fn_step(t)
        return (jnp.where(is_ctx | (ph == 1), gt, gt - tile), 0)

    def tab_row(t):
        is_ctx, ph, tile, _ = _fn_step(t)
        return (jnp.where(is_ctx | (ph == 0), 0, tile), 0)

    return pl.pallas_call(
        _fn_kernel,
        out_shape=jax.ShapeDtypeStruct((N_TOK, D), F32),
        grid=(FN_STEPS,),
        in_specs=[pl.BlockSpec((TM, D), x_tile),
                  pl.BlockSpec((1, 6, D), lambda t: (_fn_step(t)[3], 0, 0)),
                  pl.BlockSpec((1, D), const),
                  pl.BlockSpec((FN_GW, 2 * FN_GW), const),
                  pl.BlockSpec((CTX_T, 2 * CTX_T), const),
                  pl.BlockSpec((TM, 2 * LAT_T), tab_row),
                  pl.BlockSpec((D, D), const)],
        out_specs=pl.BlockSpec((TM, D), out_tile),
        scratch_shapes=[pltpu.VMEM((2 * LAT_T, D), BF16), pltpu.VMEM((2 * LAT_T, D), BF16)],
        compiler_params=_cp(1, VMEM_LIMIT),
    )(x, modt, norm_g, wc, tc, tl, w_out)


@functools.lru_cache(maxsize=None)
def _dft_tables():
    def cos_sin(n):
        k = np.arange(n, dtype=np.int64)
        ang = ((k[:, None] * k[None, :]) % n).astype(np.float64) * (2.0 * np.pi / n)
        return np.cos(ang), np.sin(ang)

    cc, sc = cos_sin(FN_GW)
    out = [np.concatenate([cc, sc], axis=1).astype(np.float32)]
    for t_len in (CTX_T, LAT_T):
        ct, st = cos_sin(t_len)
        out.append(np.concatenate([ct, -st], axis=1).astype(np.float32))
    return tuple(out)


def _router_kernel(x_ref, mod_ref, g_ref, rw_hi_ref, rw_lo_ref, rb_ref, h_ref, idt_ref, gate_ref, cnt_ref):
    h = _norm_mod(x_ref[...], g_ref[...], mod_ref[0, 3:4, :], mod_ref[0, 4:5, :])
    _store_packed_rows(h_ref, h)
    h_hi, h_lo = _split2(h)
    logits = _dot_split(h_hi, h_lo, rw_hi_ref[...], rw_lo_ref[...]) + rb_ref[...]
    lane = lax.broadcasted_iota(jnp.int32, logits.shape, 1)
    lane_f = lane.astype(F32)
    cur = jnp.where(lane < N_EXPERTS, logits, -jnp.inf)
    vals, idxs = [], []
    for _ in range(TOP_K):
        m = jnp.max(cur, axis=1, keepdims=True)
        ik = jnp.min(jnp.where(cur == m, lane_f, 128.0), axis=1, keepdims=True).astype(jnp.int32)
        vals.append(m)
        idxs.append(ik)
        cur = jnp.where(lane == ik, -jnp.inf, cur)
    es = [jnp.exp(v - vals[0]) for v in vals]
    denom = functools.reduce(lambda a, b: a + b, es)
    idx_out = jnp.zeros(logits.shape, F32)
    gate_out = jnp.zeros(logits.shape, F32)
    for kk in range(TOP_K):
        idx_out = jnp.where(lane == kk, idxs[kk].astype(F32), idx_out)
        gate_out = jnp.where(lane == kk, es[kk] / denom, gate_out)
    idt_ref[...] = jnp.transpose(idx_out)[0:8, :].astype(jnp.int32)
    gate_ref[...] = gate_out

    @pl.when(pl.program_id(0) == 0)
    def _():
        cnt_ref[...] = jnp.zeros(cnt_ref.shape, F32)

    hits = functools.reduce(lambda a, b: a + b, [jnp.where(lane == ik, 1.0, 0.0) for ik in idxs])
    cnt_ref[...] += jnp.sum(hits, axis=0, keepdims=True)


def _router(x, modt, norm_g, rw_hi, rw_lo, rb):
    tile = lambda i: (i, 0)
    const = lambda i: (0, 0)
    return pl.pallas_call(
        _router_kernel,
        out_shape=(jax.ShapeDtypeStruct((N_TOK * PACK_CHUNKS, 128), jnp.uint32),
                   jax.ShapeDtypeStruct((8, N_TOK), jnp.int32),
                   jax.ShapeDtypeStruct((N_TOK, 128), F32),
                   jax.ShapeDtypeStruct((1, 128), F32)),
        grid=(N_TILES,),
        in_specs=[pl.BlockSpec((TM, D), tile), pl.BlockSpec((1, 6, D), lambda i: (i, 0, 0)),
                  pl.BlockSpec((1, D), const), pl.BlockSpec((D, 128), const), pl.BlockSpec((D, 128), const),
                  pl.BlockSpec((1, 128), const)],
        out_specs=(pl.BlockSpec((TM * PACK_CHUNKS, 128), tile), pl.BlockSpec((8, TM), lambda i: (0, i)),
                   pl.BlockSpec((TM, 128), tile), pl.BlockSpec((1, 128), const)),
        compiler_params=_cp(1, VMEM_LIMIT),
    )(x, modt, norm_g, rw_hi, rw_lo, rb)


def _route_plan(idt, cnt):
    counts = cnt[0, 0:N_EXPERTS].astype(jnp.int32)
    nblk = (counts + MOE_BM - 1) // MOE_BM
    total = jnp.sum(nblk)
    pad = nblk * MOE_BM - counts
    experts = jnp.arange(N_EXPERTS, dtype=jnp.int32)
    filler = jnp.where(jnp.arange(MOE_BM - 1, dtype=jnp.int32)[None, :] < pad[:, None], experts[:, None], N_EXPERTS)
    n_tail = MOE_STEPS * MOE_BM - N_ASSIGN - N_EXPERTS * (MOE_BM - 1)
    keys = jnp.concatenate([idt[0:TOP_K].reshape(-1), filler.reshape(-1), jnp.full((n_tail,), N_EXPERTS, jnp.int32)])
    packed = lax.sort(keys * 65536 + jnp.arange(keys.shape[0], dtype=jnp.int32))
    order = packed & 65535
    b = jnp.arange(MOE_STEPS, dtype=jnp.int32)
    be = packed[::MOE_BM] >> 16
    be = jnp.where(b < total, be, be[jnp.maximum(total - 1, 0)])
    first = (((b == 0) | (be != jnp.concatenate([be[:1], be[:-1]]))) & (b < total)).astype(jnp.int32)
    return be, first, total.reshape(1), order


def _expert_kernel(be_ref, first_ref, total_ref, order_ref,
                   x_hbm, w1_ref, b1_ref, w2_ref, b2_ref, y4_hbm,
                   xs_buf, y_buf, y_acc, xs_bf, w1b, w2b, gsem, ssem):
    b = pl.program_id(0)
    total = total_ref[0]

    def tile_rows(r, per_row=LANE_CHUNKS):
        return pl.ds(pl.multiple_of(r * per_row, per_row), per_row)

    def slot_rows(slot, per_row=LANE_CHUNKS):
        return pl.ds(pl.multiple_of(slot * (MOE_BM * per_row), MOE_BM * per_row), MOE_BM * per_row)

    def gather_start(pos0, slot, row0, i):
        a = order_ref[pos0 + row0 + i]
        pltpu.make_async_copy(x_hbm.at[tile_rows(a & (N_TOK - 1), PACK_CHUNKS), :],
                              xs_buf.at[tile_rows(slot * MOE_BM + row0 + i, PACK_CHUNKS), :], gsem.at[slot]).start()

    def scatter_start(pos0, slot, row0, i):
        row = row0 + i
        a = order_ref[pos0 + row]
        dst = jnp.where(a < N_ASSIGN, a, N_ASSIGN + slot * MOE_BM + row)
        pltpu.make_async_copy(y_buf.at[tile_rows(slot * MOE_BM + row), :], y4_hbm.at[tile_rows(dst), :],
                              ssem.at[slot]).start(priority=1)

    def wait_gather(slot):
        pltpu.make_async_copy(x_hbm.at[slot_rows(0, PACK_CHUNKS), :], xs_buf.at[slot_rows(slot, PACK_CHUNKS), :],
                              gsem.at[slot]).wait()

    def wait_scatter(slot):
        pltpu.make_async_copy(y_buf.at[slot_rows(slot), :], y4_hbm.at[slot_rows(0), :], ssem.at[slot]).wait()

    def row_loop(fn):
        def body(g, c):
            for i in range(8):
                fn(g * 8, i)
            return c
        lax.fori_loop(0, MOE_BM // 8, body, 0)

    @pl.when(b == 0)
    def _():
        y_buf[...] = jnp.zeros(y_buf.shape, F32)
        for s in range(2):
            pltpu.make_async_copy(y_buf.at[slot_rows(s), :], y4_hbm.at[slot_rows(N_ASSIGN // MOE_BM + s), :],
                                  ssem.at[s]).start()
        row_loop(functools.partial(gather_start, 0, 0))
        row_loop(functools.partial(gather_start, jnp.minimum(1, total - 1) * MOE_BM, 1))

    @pl.when(b < total)
    def _():
        slot = b % MOE_SLOTS

        @pl.when(first_ref[b] == 1)
        def _():
            for j in range(2 * MOE_NHC):
                w1b[j] = w1_ref[0, 0, :, j * MOE_HC:(j + 1) * MOE_HC].astype(BF16)
            for j in range(MOE_NHC):
                w2b[j] = w2_ref[0, 0, j * MOE_HC:(j + 1) * MOE_HC, :].astype(BF16)

        wait_gather(slot)
        wait_scatter(slot)
        _load_packed_rows(xs_buf, slot * MOE_BM, MOE_BM, xs_bf)
        y_acc[...] = jnp.broadcast_to(b2_ref[0, 0], (MOE_BM, D))

        npos = jnp.minimum(b + 2, total - 1) * MOE_BM
        ppos = jnp.where(b >= 1, b - 1, MOE_STEPS - 1) * MOE_BM
        oslot = (b + 2) % MOE_SLOTS

        def hidden_chunk(j, c):
            xb = xs_bf[...]
            hg = _dot(xb, w1b[j]) + b1_ref[0, 0, pl.ds(j, 1), :]
            hu = _dot(xb, w1b[MOE_NHC + j]) + b1_ref[0, 0, pl.ds(MOE_NHC + j, 1), :]
            gate = jnp.minimum(hg, SWIGLU_LIMIT)
            up = jnp.clip(hu, -SWIGLU_LIMIT, SWIGLU_LIMIT)
            act = ((up + 1.0) * (gate * _sigmoid(SWIGLU_ALPHA * gate))).astype(BF16)
            y_acc[...] += _dot(act, w2b[j])
            for i in range(MOE_RPC):
                gather_start(npos, oslot, j * MOE_RPC, i)
                scatter_start(ppos, oslot, j * MOE_RPC, i)
            return c
        lax.fori_loop(0, MOE_NHC, hidden_chunk, 0)
        _store_token_tiles(y_buf, slot * MOE_BM, y_acc[...])

    @pl.when(b == total)
    def _():
        last = total - 1
        row_loop(functools.partial(scatter_start, last * MOE_BM, last % MOE_SLOTS))

    @pl.when(b == MOE_STEPS - 1)
    def _():
        for s in range(MOE_SLOTS):
            wait_scatter(s)
        wait_gather(total % MOE_SLOTS)
        wait_gather((total + 1) % MOE_SLOTS)


def _experts(layer, plan, h2, w1, b1, w2, b2):
    be, first, total, order = plan
    wmap = lambda b, be, *_: (layer, be[b], 0, 0)
    grid_spec = pltpu.PrefetchScalarGridSpec(
        num_scalar_prefetch=4,
        grid=(MOE_STEPS,),
        in_specs=[pl.BlockSpec(memory_space=pl.ANY),
                  pl.BlockSpec((1, 1, D, 2 * D), wmap), pl.BlockSpec((1, 1, 2 * MOE_NHC, MOE_HC), wmap),
                  pl.BlockSpec((1, 1, D, D), wmap), pl.BlockSpec((1, 1, 1, D), wmap)],
        out_specs=pl.BlockSpec(memory_space=pl.ANY),
        scratch_shapes=[pltpu.VMEM((MOE_SLOTS * MOE_BM * PACK_CHUNKS, 128), jnp.uint32),
                        pltpu.VMEM((MOE_SLOTS * MOE_BM * LANE_CHUNKS, 128), F32),
                        pltpu.VMEM((MOE_BM, D), F32), pltpu.VMEM((MOE_BM, D), BF16),
                        pltpu.VMEM((2 * MOE_NHC, D, MOE_HC), BF16), pltpu.VMEM((MOE_NHC, MOE_HC, D), BF16),
                        pltpu.SemaphoreType.DMA((MOE_SLOTS,)), pltpu.SemaphoreType.DMA((MOE_SLOTS,))],
    )
    return pl.pallas_call(
        _expert_kernel,
        out_shape=jax.ShapeDtypeStruct((Y4_ROWS * LANE_CHUNKS, 128), F32),
        grid_spec=grid_spec,
        compiler_params=_cp(1, VMEM_LIMIT, disable_bounds_checks=True),
    )(be, first, total, order, h2, w1, b1.reshape(DEPTH, N_EXPERTS, 2 * MOE_NHC, MOE_HC), w2,
      b2.reshape(DEPTH, N_EXPERTS, 1, D))


def _combine_kernel(final, x_ref, y0, y1, y2, y3, gate_ref, mod_ref, fg_ref, out_ref):
    gates = gate_ref[...]
    parts = []
    for c in range(LANE_CHUNKS):
        moe = gates[:, 0:1] * _load_token_tile_chunk(y0, 0, TM, c)
        for kk, y in ((1, y1), (2, y2), (3, y3)):
            moe = moe + gates[:, kk:kk + 1] * _load_token_tile_chunk(y, 0, TM, c)
        parts.append(moe)
    x = x_ref[...] + mod_ref[0, 5:6, :] * jnp.concatenate(parts, axis=1)
    out_ref[...] = _rms(x) * fg_ref[...] if final else x


def _combine(x, y4, gate, modt, final_g, final):
    tile = lambda i: (i, 0)
    plane = lambda kk: pl.BlockSpec((TM * LANE_CHUNKS, 128), lambda i: (kk * N_TILES + i, 0))
    return pl.pallas_call(
        functools.partial(_combine_kernel, final),
        out_shape=jax.ShapeDtypeStruct((N_TOK, D), F32),
        grid=(N_TILES,),
        in_specs=[pl.BlockSpec((TM, D), tile), plane(0), plane(1), plane(2), plane(3),
                  pl.BlockSpec((TM, 128), tile), pl.BlockSpec((1, 6, D), lambda i: (i, 0, 0)),
                  pl.BlockSpec((1, D), lambda i: (0, 0))],
        out_specs=pl.BlockSpec((TM, D), tile),
        compiler_params=_cp(1, VMEM_LIMIT),
    )(x, y4, y4, y4, y4, gate, modt, final_g)


def _moe(layer, x, modt, norm2_g, router_w, router_b, w1, b1, w2, b2, final_g, final):
    rw = jnp.pad(router_w, ((0, 0), (0, 128 - N_EXPERTS)))
    rw_hi, rw_lo = _split2(rw)
    rb = jnp.pad(router_b, (0, 128 - N_EXPERTS)).reshape(1, 128)
    h2, idt, gate, cnt = _router(x, modt, norm2_g, rw_hi, rw_lo, rb)
    y4 = _experts(layer, _route_plan(idt, cnt), h2, w1, b1, w2, b2)
    return _combine(x, y4, gate, modt, final_g, final)


def _rope_tables():
    def tables(n_tokens):
        rows = n_tokens // GRID_W
        row = jnp.repeat(jnp.arange(rows, dtype=F32), GRID_W)
        col = jnp.tile(jnp.arange(GRID_W, dtype=F32), rows)
        n_freq = QK_ROPE // 4
        inv = ROPE_THETA ** (-jnp.arange(n_freq, dtype=F32) / n_freq)
        ang = jnp.stack([row[:, None] * inv, col[:, None] * inv], axis=1)
        return jnp.cos(ang), jnp.sin(ang)

    cos, sin = tables(LAT_T)
    cos64 = jnp.concatenate([cos[:, 0], cos[:, 0], cos[:, 1], cos[:, 1]], axis=1)
    sin64 = jnp.concatenate([-sin[:, 0], sin[:, 0], -sin[:, 1], sin[:, 1]], axis=1)
    cos_all = jnp.concatenate([jnp.ones((N_CTX, QK_ROPE), F32)] + [cos64] * N_LAT_SEQ, axis=0)
    sin_all = jnp.concatenate([jnp.zeros((N_CTX, QK_ROPE), F32)] + [sin64] * N_LAT_SEQ, axis=0)
    return jnp.tile(cos_all, (1, 2)), jnp.tile(sin_all, (1, 2))


_PAIR_SWAP = np.concatenate([np.arange(16, 32), np.arange(0, 16), np.arange(48, 64), np.arange(32, 48)])


def _mla_weights(w_in, w_uq, w_ukv):
    kpe = w_in[:, Q_LORA + KV_LORA:]
    pad = jnp.zeros((D, 64), F32)
    w_in2 = jnp.concatenate([w_in[:, :Q_LORA + KV_LORA], kpe, pad, kpe[:, _PAIR_SWAP], pad], axis=1)
    uq = w_uq.reshape(Q_LORA, MLA_HEADS, QK_NOPE + QK_ROPE)
    rope = uq[:, :, QK_NOPE:]
    zpad = jnp.zeros((Q_LORA, MLA_HEADS, 128 - QK_ROPE), F32)
    w_uq2 = jnp.concatenate([uq[:, :, :QK_NOPE].reshape(Q_LORA, NOPE_ALL),
                             jnp.concatenate([rope, zpad], axis=2).reshape(Q_LORA, ROPE_PAD_ALL),
                             jnp.concatenate([rope[:, :, _PAIR_SWAP], zpad], axis=2).reshape(Q_LORA, ROPE_PAD_ALL)],
                            axis=1)
    ukv = w_ukv.reshape(KV_LORA, MLA_HEADS, QK_NOPE + V_HEAD)
    w_ukv2 = jnp.concatenate([ukv[:, :, :QK_NOPE].reshape(KV_LORA, NOPE_ALL),
                              ukv[:, :, QK_NOPE:].reshape(KV_LORA, NOPE_ALL)], axis=1)
    return w_in2.astype(BF16), w_uq2.astype(BF16), w_ukv2.astype(BF16)


def _ml_init_state(state_c, state_n, state_m, j, direction):
    c = state_c[:, j, direction].astype(F32)
    n = state_n[:, j, direction].astype(F32)[..., None]
    m = jnp.broadcast_to(state_m[:, j, direction].astype(F32)[..., None, None], n.shape)
    pad = jnp.zeros(c.shape[:-1] + (ML_AUG - ML_DV - 2,), F32)
    lat = jnp.concatenate([c, n, m, pad], axis=-1)
    return jnp.concatenate([jnp.zeros((N_CTX_SEQ,) + lat.shape[1:], F32), lat], axis=0)


def kernel(x_prompt, x_sample, cache_mla_ckv, cache_mla_kpe, state_mlstm_C, state_mlstm_n, state_mlstm_m, c, c_ctx, norm1_g, norm2_g, ada_w, ada_b, ml_w_in, ml_gate_b, ml_norm_g, ml_w_out, mla_w_in, mla_q_norm_g, mla_w_uq, mla_kv_norm_g, mla_w_ukv, mla_w_out, fn_w_out, router_w, router_b, exp_w1, exp_b1, exp_w2, exp_b2, final_g):
    x = jnp.concatenate([x_prompt.reshape(N_CTX, D), x_sample.reshape(N_LAT_SEQ * LAT_T, D)], axis=0)

    cond8 = jnp.concatenate([c_ctx[None, :], c, jnp.zeros((8 - 1 - N_LAT_SEQ, D), F32)], axis=0)
    mod = _modulation(cond8, ada_w, ada_b)
    tile_cond = np.concatenate([np.zeros(CTX_TILES, np.int32)] +
                               [np.full(LAT_TILES, 1 + s, np.int32) for s in range(N_LAT_SEQ)])
    modt = mod[:, tile_cond].reshape(DEPTH, N_TILES, 6, D)

    hk = ML_HEADS * ML_DK
    states = []
    new_ckv = new_kpe = None
    for l in range(DEPTH):
        kind, j = l % 3, l // 3
        n1 = norm1_g[l].reshape(1, D)
        if kind == 0:
            w = ml_w_in[j]
            w_gates = w[:, 2 * hk + 2 * D:]
            qkv, o, gg, gt = _ml_in(x, modt[l], n1, w[:, :2 * hk + 2 * D].astype(BF16),
                                    jnp.pad(w_gates, ((0, 0), (0, 128 - 4 * ML_HEADS))).astype(BF16),
                                    w_gates.T.astype(BF16))
            gb = ml_gate_b[j].reshape(4 * ML_HEADS).astype(F32)
            gb_row = jnp.pad(gb, (0, 128 - 4 * ML_HEADS)).reshape(1, 128)
            gb_col = jnp.broadcast_to(gb[:, None], (4 * ML_HEADS, 128))
            hf, hb, st_f, st_b = _ml_core(qkv, gg, gt, gb_row, gb_col,
                                          _ml_init_state(state_mlstm_C, state_mlstm_n, state_mlstm_m, j, 0),
                                          _ml_init_state(state_mlstm_C, state_mlstm_n, state_mlstm_m, j, 1))
            states.append((st_f[:N_CTX_SEQ], st_b[:N_CTX_SEQ]))
            x = _ml_out(hf, hb, o, ml_norm_g[j].reshape(1, D), ml_w_out[j].astype(BF16), x, modt[l])
        elif kind == 1:
            w_in2, w_uq2, w_ukv2 = _mla_weights(mla_w_in[j], mla_w_uq[j], mla_w_ukv[j])
            cos2, sin2 = _rope_tables()
            q, ckv, kpe = _mla_in(x, modt[l], n1, w_in2, mla_q_norm_g[j].reshape(1, Q_LORA),
                                  mla_kv_norm_g[j].reshape(1, KV_LORA), w_uq2, cos2, sin2)
            new_ckv, new_kpe = ckv[:N_CTX], kpe[:N_CTX, 0:QK_ROPE]
            k_tok, v_tok = _kv_expand(ckv, kpe, w_ukv2)
            kr_past = jnp.pad(cache_mla_kpe[:, j].reshape(N_LAT_SEQ * PAST_LEN, QK_ROPE), ((0, 0), (0, 128 - QK_ROPE)))
            k_past, v_past = _kv_expand(cache_mla_ckv[:, j].reshape(N_LAT_SEQ * PAST_LEN, KV_LORA), kr_past, w_ukv2)
            att = _attention(q, k_tok, v_tok, k_past, v_past)
            x = _res_linear(att, mla_w_out[j].astype(BF16), x, modt[l])
        else:
            x = _fourier(x, modt[l], n1, _dft_tables(), fn_w_out[j].astype(BF16))
        x = _moe(l, x, modt[l], norm2_g[l].reshape(1, D), router_w[l], router_b[l],
                 exp_w1, exp_b1, exp_w2, exp_b2, final_g.reshape(1, D), l == DEPTH - 1)

    y_prompt = x[:N_CTX].reshape(N_CTX_SEQ, CTX_T, D)
    y_sample = x[N_CTX:].reshape(N_LAT_SEQ, LAT_T, D)
    new_mla_ckv = new_ckv.reshape(N_CTX_SEQ, 1, CTX_T, KV_LORA)
    new_mla_kpe = new_kpe.reshape(N_CTX_SEQ, 1, CTX_T, QK_ROPE)
    st = jnp.stack([jnp.stack([sf, sb], axis=1) for sf, sb in states], axis=1)
    new_c = st[..., 0:ML_DV]
    new_n = st[..., ML_N_LANE]
    new_m = st[..., 0, ML_M_LANE]
    return (y_prompt, y_sample, new_mla_ckv, new_mla_kpe, new_c, new_n, new_m)
```

```python
import functools

import numpy as np
import jax
import jax.numpy as jnp
from jax import lax
from jax.experimental import pallas as pl
from jax.experimental.pallas import tpu as pltpu

F32 = jnp.float32
BF16 = jnp.bfloat16

D = 1024
DEPTH = 4
N_CTX_SEQ, CTX_T = 16, 256
N_LAT_SEQ, LAT_T = 2, 2048
N_CTX = N_CTX_SEQ * CTX_T
N_TOK = N_CTX + N_LAT_SEQ * LAT_T
TM = 256
N_TILES = N_TOK // TM
CTX_TILES = N_CTX // TM
LAT_TILES = LAT_T // TM
N_SEQ = N_CTX_SEQ + N_LAT_SEQ
PAST_LEN = 512
GRID_W = 64
RMS_EPS = 1e-6

ML_HEADS, ML_DK, ML_DV = 8, 64, 128
ML_AUG = 256
ML_N_LANE, ML_M_LANE = ML_DV, ML_DV + 1

MLA_HEADS, QK_NOPE, QK_ROPE, V_HEAD = 16, 128, 64, 128
Q_LORA, KV_LORA = 384, 256
ROPE_THETA = 10000.0
ATT_HG = 4
ATT_SCALE = (QK_NOPE + QK_ROPE) ** -0.5

FN_GROUPS, FN_GW = 4, 256

N_EXPERTS, TOP_K = 32, 4
SWIGLU_ALPHA, SWIGLU_LIMIT = 1.702, 7.0
MOE_BM = 256
N_ASSIGN = N_TOK * TOP_K
MOE_NB = N_ASSIGN // MOE_BM + N_EXPERTS
MOE_STEPS = MOE_NB + 1
MOE_SLOTS = 3
MOE_HC = 256
MOE_NHC = D // MOE_HC
MOE_RPC = MOE_BM // MOE_NHC
Y4_ROWS = N_ASSIGN + MOE_SLOTS * MOE_BM

VMEM_LIMIT = 56 * 1024 * 1024


def _cp(n_grid_axes, vmem=None, **kw):
    return pltpu.CompilerParams(dimension_semantics=("arbitrary",) * n_grid_axes, vmem_limit_bytes=vmem, **kw)


def _dot(a, b):
    return jnp.dot(a, b, preferred_element_type=F32)


def _dot_nt(a, b):
    return lax.dot_general(a, b, (((1,), (1,)), ((), ())), preferred_element_type=F32)


def _dot_tn(a, b):
    return lax.dot_general(a, b, (((0,), (0,)), ((), ())), preferred_element_type=F32)


def _sigmoid(x):
    return 1.0 / (1.0 + jnp.exp(-x))


def _log_sigmoid(x):
    return jnp.minimum(x, 0.0) - jnp.log(1.0 + jnp.exp(-jnp.abs(x)))


def _rms(x):
    return x * lax.rsqrt(jnp.mean(x * x, axis=-1, keepdims=True) + RMS_EPS)


def _norm_mod(x, g, shift, scale):
    return (_rms(x) * g) * (1.0 + scale) + shift


def _split2(x):
    hi = x.astype(BF16)
    lo = (x - hi.astype(F32)).astype(BF16)
    return hi, lo


def _split3(x):
    hi = x.astype(BF16)
    r = x - hi.astype(F32)
    mid = r.astype(BF16)
    lo = (r - mid.astype(F32)).astype(BF16)
    return hi, mid, lo


LANE_CHUNKS = D // 128


def _store_token_tiles(ref, row0, val):
    n = val.shape[0]
    for c in range(LANE_CHUNKS):
        ref[pl.ds(row0 * LANE_CHUNKS + c, n, stride=LANE_CHUNKS), :] = val[:, c * 128:(c + 1) * 128]


def _load_token_tile_chunk(ref, row0, n, c):
    return ref[pl.ds(row0 * LANE_CHUNKS + c, n, stride=LANE_CHUNKS), :]


PACK_CHUNKS = LANE_CHUNKS // 2
HI16 = 0xFFFF0000


def _store_packed_rows(ref, val):
    n = val.shape[0]
    for c in range(PACK_CHUNKS):
        lo = pltpu.bitcast(val[:, c * 128:(c + 1) * 128].astype(BF16).astype(F32), jnp.uint32)
        hi = pltpu.bitcast(val[:, D // 2 + c * 128:D // 2 + (c + 1) * 128].astype(BF16).astype(F32), jnp.uint32)
        ref[pl.ds(c, n, stride=PACK_CHUNKS), :] = (hi & jnp.uint32(HI16)) | (lo >> jnp.uint32(16))


def _load_packed_rows(ref, row0, n, out_ref):
    for c in range(PACK_CHUNKS):
        u = ref[pl.ds(row0 * PACK_CHUNKS + c, n, stride=PACK_CHUNKS), :]
        out_ref[:, c * 128:(c + 1) * 128] = pltpu.bitcast(u << jnp.uint32(16), F32).astype(BF16)
        out_ref[:, D // 2 + c * 128:D // 2 + (c + 1) * 128] = pltpu.bitcast(u & jnp.uint32(HI16), F32).astype(BF16)


def _tile_seq(r):
    return jnp.where(r < CTX_TILES, r, CTX_TILES + (r - CTX_TILES) // LAT_TILES)


def _lat_seq(r):
    return jnp.clip((r - CTX_TILES) // LAT_TILES, 0, N_LAT_SEQ - 1)


MOD_TN = 1536


def _mod_kernel(c_ref, w_ref, b_ref, o_ref):
    a = c_ref[...]
    s_hi, s_lo = _split2(a * _sigmoid(a))
    w_hi, w_lo = _split2(w_ref[0])
    o_ref[0] = _dot_split(s_hi, s_lo, w_hi, w_lo) + b_ref[0]


def _modulation(cond8, ada_w, ada_b):
    n_col = ada_w.shape[-1]
    return pl.pallas_call(
        _mod_kernel,
        out_shape=jax.ShapeDtypeStruct((DEPTH, 8, n_col), F32),
        grid=(DEPTH, n_col // MOD_TN),
        in_specs=[pl.BlockSpec((8, D), lambda l, j: (0, 0)),
                  pl.BlockSpec((1, D, MOD_TN), lambda l, j: (l, 0, j)),
                  pl.BlockSpec((1, 1, MOD_TN), lambda l, j: (l, 0, j))],
        out_specs=pl.BlockSpec((1, 8, MOD_TN), lambda l, j: (l, 0, j)),
        compiler_params=_cp(2, VMEM_LIMIT),
    )(cond8, ada_w, ada_b.reshape(DEPTH, 1, n_col))


def _ml_in_kernel(x_ref, mod_ref, g_ref, w_ref, wg_ref, wgt_ref, qkv_ref, o_ref, gg_ref, gt_ref):
    h = _norm_mod(x_ref[...], g_ref[...], mod_ref[0, 0:1, :], mod_ref[0, 1:2, :]).astype(BF16)
    hk = ML_HEADS * ML_DK
    lane = lax.broadcasted_iota(jnp.int32, (1, 2 * hk), 1)
    qscale = jnp.where(lane < hk, ML_DK ** -0.5, 1.0).astype(F32)
    qkv_ref[:, 0:2 * hk] = (_dot(h, w_ref[:, 0:2 * hk]) * qscale).astype(BF16)
    qkv_ref[:, 2 * hk:2 * hk + D] = _dot(h, w_ref[:, 2 * hk:2 * hk + D]).astype(BF16)
    o_ref[...] = _dot(h, w_ref[:, 2 * hk + D:2 * hk + 2 * D])
    gg_ref[...] = _dot(h, wg_ref[...])
    gt_ref[...] = _dot_nt(wgt_ref[...], h)


def _ml_in(x, modt, norm_g, w_main, w_g, w_gt):
    tile = lambda i: (i, 0)
    const = lambda i: (0, 0)
    return pl.pallas_call(
        _ml_in_kernel,
        out_shape=(jax.ShapeDtypeStruct((N_TOK, 2 * D), BF16),
                   jax.ShapeDtypeStruct((N_TOK, D), F32),
                   jax.ShapeDtypeStruct((N_TOK, 128), F32),
                   jax.ShapeDtypeStruct((32, N_TOK), F32)),
        grid=(N_TILES,),
        in_specs=[pl.BlockSpec((TM, D), tile),
                  pl.BlockSpec((1, 6, D), lambda i: (i, 0, 0)),
                  pl.BlockSpec((1, D), const),
                  pl.BlockSpec((D, 3 * D), const),
                  pl.BlockSpec((D, 128), const),
                  pl.BlockSpec((32, D), const)],
        out_specs=(pl.BlockSpec((TM, 2 * D), tile),
                   pl.BlockSpec((TM, D), tile),
                   pl.BlockSpec((TM, 128), tile),
                   pl.BlockSpec((32, TM), lambda i: (0, i))),
        compiler_params=_cp(1, VMEM_LIMIT),
    )(x, modt, norm_g, w_main, w_g, w_gt)


def _ml_direction(lower, q_ref, k_ref, v_ref, g_ref, gt_ref, gbrow_ref, gbcol_ref, h_ref, st_ref, c_s, m_s):
    L = TM
    d_off = 0 if lower else ML_HEADS
    row = lax.broadcasted_iota(jnp.int32, (L, L), 0)
    col = lax.broadcasted_iota(jnp.int32, (L, L), 1)
    mask = (col <= row) if lower else (col >= row)
    tri = jnp.where(mask, 1.0, 0.0).astype(BF16)

    lane = lax.broadcasted_iota(jnp.int32, (L, 128), 1)
    gc = g_ref[...] + gbrow_ref[...]
    gc = jnp.where((lane >= 2 * ML_HEADS) & (lane < 4 * ML_HEADS), _log_sigmoid(gc), gc)
    bc = sum(_dot(tri, p) for p in _split3(gc))
    sub = lax.broadcasted_iota(jnp.int32, (32, L), 0)
    gr = gt_ref[...] + gbcol_ref[:, 0:1]
    gr = jnp.where(sub >= 2 * ML_HEADS, _log_sigmoid(gr), gr)
    br = sum(_dot_nt(p, tri) for p in _split3(gr))

    q = q_ref[...]
    k = k_ref[...]
    v = v_ref[...]
    lane_a = lax.broadcasted_iota(jnp.int32, (L, ML_AUG - ML_DV), 1)
    ones_blk = jnp.where(lane_a == 0, 1.0, 0.0).astype(BF16)
    lane_s = lax.broadcasted_iota(jnp.int32, (ML_DK, ML_AUG), 1)

    for hd in range(ML_HEADS):
        ci = d_off + hd
        cf = 2 * ML_HEADS + d_off + hd
        i_col, b_col = gc[:, ci:ci + 1], bc[:, cf:cf + 1]
        i_row, b_row = gr[ci:ci + 1, :], br[cf:cf + 1, :]
        total = b_col[L - 1:L, :] if lower else b_col[0:1, :]
        m = m_s[hd][0:1, 0:1]
        c_aug = c_s[hd]
        rmat = jnp.where(mask, i_row - b_row, -jnp.inf)
        big_m = jnp.max(rmat, axis=1, keepdims=True)
        mm = jnp.maximum(m, big_m)
        qh = q[:, hd * ML_DK:(hd + 1) * ML_DK]
        kh = k[:, hd * ML_DK:(hd + 1) * ML_DK]
        v_aug = jnp.concatenate([v[:, hd * ML_DV:(hd + 1) * ML_DV], ones_blk], axis=1)
        p = (_dot_nt(qh, kh) * jnp.exp(rmat - mm)).astype(BF16)
        num = _dot(p, v_aug) + jnp.exp(m - mm) * _dot(qh, c_aug.astype(BF16))
        den = jnp.maximum(jnp.abs(num[:, ML_N_LANE:ML_N_LANE + 1]), jnp.exp(-(b_col + mm)))
        h_ref[:, hd * ML_DV:(hd + 1) * ML_DV] = num[:, 0:ML_DV] / den

        g_col = (total - b_col) + i_col
        m_new = jnp.maximum(total + m, jnp.max(g_col, axis=0, keepdims=True))
        kw = (kh.astype(F32) * jnp.exp(g_col - m_new)).astype(BF16)
        c_new = jnp.exp(total + m - m_new) * c_aug + _dot_tn(kw, v_aug)
        c_s[hd] = c_new
        m_s[hd] = jnp.broadcast_to(m_new, (8, 128))
        st_ref[0, hd] = jnp.where(lane_s == ML_M_LANE, m_new, c_new)


def _ml_core_kernel(qf, kf, vf, gf, gtf, qb, kb, vb, gb, gtb, gbrow, gbcol, initf, initb,
                    hf_ref, hb_ref, stf_ref, stb_ref, cf_s, cb_s, mf_s, mb_s):
    i = pl.program_id(0)
    start_f = (i <= CTX_TILES) | (i == CTX_TILES + LAT_TILES)
    start_b = (i == 0) | (i == LAT_TILES) | (i >= 2 * LAT_TILES)

    def load_state(init_ref, c_s, m_s):
        c_s[...] = init_ref[0]
        for hd in range(ML_HEADS):
            m_s[hd] = jnp.broadcast_to(init_ref[0, hd][0:1, ML_M_LANE:ML_M_LANE + 1], (8, 128))

    @pl.when(start_f)
    def _():
        load_state(initf, cf_s, mf_s)

    @pl.when(start_b)
    def _():
        load_state(initb, cb_s, mb_s)

    _ml_direction(True, qf, kf, vf, gf, gtf, gbrow, gbcol, hf_ref, stf_ref, cf_s, mf_s)
    _ml_direction(False, qb, kb, vb, gb, gtb, gbrow, gbcol, hb_ref, stb_ref, cb_s, mb_s)


def _ml_core(qkv, gg, gt, gb_row, gb_col, init_f, init_b):
    last = N_TILES - 1
    fwd = lambda i: i
    bwd = lambda i: last - i
    hk = ML_HEADS * ML_DK

    def specs(t):
        return [pl.BlockSpec((TM, hk), lambda i: (t(i), 0)),
                pl.BlockSpec((TM, hk), lambda i: (t(i), 1)),
                pl.BlockSpec((TM, D), lambda i: (t(i), 1)),
                pl.BlockSpec((TM, 128), lambda i: (t(i), 0)),
                pl.BlockSpec((32, TM), lambda i: (0, t(i)))]

    def st_spec(t):
        return pl.BlockSpec((1, ML_HEADS, ML_DK, ML_AUG), lambda i: (_tile_seq(t(i)), 0, 0, 0))

    st_shape = jax.ShapeDtypeStruct((N_SEQ, ML_HEADS, ML_DK, ML_AUG), F32)
    return pl.pallas_call(
        _ml_core_kernel,
        out_shape=(jax.ShapeDtypeStruct((N_TOK, D), F32), jax.ShapeDtypeStruct((N_TOK, D), F32),
                   st_shape, st_shape),
        grid=(N_TILES,),
        in_specs=specs(fwd) + specs(bwd) + [
            pl.BlockSpec((1, 128), lambda i: (0, 0)),
            pl.BlockSpec((32, 128), lambda i: (0, 0)),
            st_spec(fwd), st_spec(bwd)],
        out_specs=(pl.BlockSpec((TM, D), lambda i: (i, 0)),
                   pl.BlockSpec((TM, D), lambda i: (last - i, 0)),
                   st_spec(fwd), st_spec(bwd)),
        scratch_shapes=[pltpu.VMEM((ML_HEADS, ML_DK, ML_AUG), F32),
                        pltpu.VMEM((ML_HEADS, ML_DK, ML_AUG), F32),
                        pltpu.VMEM((ML_HEADS, 8, 128), F32),
                        pltpu.VMEM((ML_HEADS, 8, 128), F32)],
        compiler_params=_cp(1, VMEM_LIMIT),
    )(qkv, qkv, qkv, gg, gt, qkv, qkv, qkv, gg, gt, gb_row, gb_col, init_f, init_b)


def _ml_out_kernel(hf_ref, hb_ref, o_ref, ng_ref, w_ref, x_ref, mod_ref, out_ref, z_s):
    hh = hf_ref[...] + hb_ref[...]
    o = o_ref[...]
    for hd in range(ML_HEADS):
        sl = slice(hd * ML_DV, (hd + 1) * ML_DV)
        z_s[:, sl] = (_sigmoid(o[:, sl]) * (_rms(hh[:, sl]) * ng_ref[:, sl])).astype(BF16)
    out_ref[...] = x_ref[...] + mod_ref[0, 2:3, :] * _dot(z_s[...], w_ref[...])


def _ml_out(hf, hb, o, norm_g, w_out, x, modt):
    tile = lambda i: (i, 0)
    const = lambda i: (0, 0)
    return pl.pallas_call(
        _ml_out_kernel,
        out_shape=jax.ShapeDtypeStruct((N_TOK, D), F32),
        grid=(N_TILES,),
        in_specs=[pl.BlockSpec((TM, D), tile), pl.BlockSpec((TM, D), tile), pl.BlockSpec((TM, D), tile),
                  pl.BlockSpec((1, D), const), pl.BlockSpec((D, D), const),
                  pl.BlockSpec((TM, D), tile), pl.BlockSpec((1, 6, D), lambda i: (i, 0, 0))],
        out_specs=pl.BlockSpec((TM, D), tile),
        scratch_shapes=[pltpu.VMEM((TM, D), BF16)],
        compiler_params=_cp(1, VMEM_LIMIT),
    )(hf, hb, o, norm_g, w_out, x, modt)


MLA_IN_COLS = 896
NOPE_ALL = MLA_HEADS * QK_NOPE
ROPE_ALL = MLA_HEADS * QK_ROPE


QK_CAT = 256
QCAT_ALL = MLA_HEADS * QK_CAT
ROPE_PAD_ALL = MLA_HEADS * 128


def _mla_in_kernel(x_ref, mod_ref, g_ref, w_ref, qg_ref, kvg_ref, wuq_ref, cos_ref, sin_ref,
                   q_ref, ckv_ref, kpe_ref):
    h = _norm_mod(x_ref[...], g_ref[...], mod_ref[0, 0:1, :], mod_ref[0, 1:2, :]).astype(BF16)
    c = _dot(h, w_ref[...])
    cos, sin = cos_ref[...], sin_ref[...]
    ckv_ref[...] = _rms(c[:, Q_LORA:Q_LORA + KV_LORA]) * kvg_ref[...]
    kpe_ref[...] = c[:, 640:768] * cos + c[:, 768:896] * sin
    cq = (_rms(c[:, 0:Q_LORA]) * qg_ref[...]).astype(BF16)
    nope = _dot(cq, wuq_ref[:, 0:NOPE_ALL])
    rope = _dot(cq, wuq_ref[:, NOPE_ALL:NOPE_ALL + ROPE_PAD_ALL])
    swapped = _dot(cq, wuq_ref[:, NOPE_ALL + ROPE_PAD_ALL:NOPE_ALL + 2 * ROPE_PAD_ALL])
    for hd in range(MLA_HEADS):
        sl = slice(hd * 128, (hd + 1) * 128)
        q_ref[:, hd * QK_CAT:hd * QK_CAT + QK_NOPE] = nope[:, sl].astype(BF16)
        q_ref[:, hd * QK_CAT + QK_NOPE:(hd + 1) * QK_CAT] = (rope[:, sl] * cos + swapped[:, sl] * sin).astype(BF16)


def _mla_in(x, modt, norm_g, w_in2, qg, kvg, w_uq2, cos2, sin2):
    tile = lambda i: (i, 0)
    const = lambda i: (0, 0)
    return pl.pallas_call(
        _mla_in_kernel,
        out_shape=(jax.ShapeDtypeStruct((N_TOK, QCAT_ALL), BF16),
                   jax.ShapeDtypeStruct((N_TOK, KV_LORA), F32),
                   jax.ShapeDtypeStruct((N_TOK, 128), F32)),
        grid=(N_TILES,),
        in_specs=[pl.BlockSpec((TM, D), tile), pl.BlockSpec((1, 6, D), lambda i: (i, 0, 0)),
                  pl.BlockSpec((1, D), const), pl.BlockSpec((D, MLA_IN_COLS), const),
                  pl.BlockSpec((1, Q_LORA), const), pl.BlockSpec((1, KV_LORA), const),
                  pl.BlockSpec((Q_LORA, NOPE_ALL + 2 * ROPE_PAD_ALL), const),
                  pl.BlockSpec((TM, 128), tile), pl.BlockSpec((TM, 128), tile)],
        out_specs=(pl.BlockSpec((TM, QCAT_ALL), tile),
                   pl.BlockSpec((TM, KV_LORA), tile), pl.BlockSpec((TM, 128), tile)),
        compiler_params=_cp(1, VMEM_LIMIT),
    )(x, modt, norm_g, w_in2, qg, kvg, w_uq2, cos2, sin2)


def _kv_expand_kernel(c_ref, kr_ref, w_ref, k_ref, v_ref):
    c = c_ref[...].astype(BF16)
    kn = _dot(c, w_ref[:, 0:NOPE_ALL])
    kr = kr_ref[...].astype(BF16)
    for hd in range(MLA_HEADS):
        k_ref[:, hd * QK_CAT:hd * QK_CAT + QK_NOPE] = kn[:, hd * QK_NOPE:(hd + 1) * QK_NOPE].astype(BF16)
        k_ref[:, hd * QK_CAT + QK_NOPE:(hd + 1) * QK_CAT] = kr
    v_ref[...] = _dot(c, w_ref[:, NOPE_ALL:2 * NOPE_ALL]).astype(BF16)


def _kv_expand(ckv, kr, w_ukv2):
    rows = ckv.shape[0]
    tile = lambda i: (i, 0)
    return pl.pallas_call(
        _kv_expand_kernel,
        out_shape=(jax.ShapeDtypeStruct((rows, QCAT_ALL), BF16), jax.ShapeDtypeStruct((rows, NOPE_ALL), BF16)),
        grid=(rows // TM,),
        in_specs=[pl.BlockSpec((TM, KV_LORA), tile), pl.BlockSpec((TM, 128), tile),
                  pl.BlockSpec((KV_LORA, 2 * NOPE_ALL), lambda i: (0, 0))],
        out_specs=(pl.BlockSpec((TM, QCAT_ALL), tile), pl.BlockSpec((TM, NOPE_ALL), tile)),
        compiler_params=_cp(1, VMEM_LIMIT),
    )(ckv, kr, w_ukv2)


def _attend(q_ref, segs, out_ref):
    for hd in range(ATT_HG):
        ksl = slice(hd * QK_CAT, (hd + 1) * QK_CAT)
        vsl = slice(hd * V_HEAD, (hd + 1) * V_HEAD)
        q = q_ref[:, ksl]
        scores = [_dot_nt(q, k_ref[:, ksl]) * ATT_SCALE for k_ref, _ in segs]
        m = functools.reduce(jnp.maximum, [jnp.max(s, axis=1, keepdims=True) for s in scores])
        ps = [jnp.exp(s - m) for s in scores]
        denom = sum(jnp.sum(p, axis=1, keepdims=True) for p in ps)
        acc = sum(_dot(p.astype(BF16), v_ref[:, vsl]) for p, (_, v_ref) in zip(ps, segs))
        out_ref[:, vsl] = (acc / denom).astype(BF16)


def _attn_kernel(q_ref, kc_ref, vc_ref, kl_ref, vl_ref, kp_ref, vp_ref, out_ref):
    r = pl.program_id(1)

    @pl.when(r < CTX_TILES)
    def _():
        _attend(q_ref, [(kc_ref, vc_ref)], out_ref)

    @pl.when(r >= CTX_TILES)
    def _():
        _attend(q_ref, [(kp_ref, vp_ref), (kl_ref, vl_ref)], out_ref)


def _attention(q, k_tok, v_tok, k_past, v_past):
    kw, vw = ATT_HG * QK_CAT, ATT_HG * V_HEAD
    ctx_t = lambda g, r: (jnp.minimum(r, CTX_TILES - 1), g)
    lat_t = lambda g, r: (N_CTX // LAT_T + _lat_seq(r), g)
    past_t = lambda g, r: (_lat_seq(r), g)
    return pl.pallas_call(
        _attn_kernel,
        out_shape=jax.ShapeDtypeStruct((N_TOK, NOPE_ALL), BF16),
        grid=(MLA_HEADS // ATT_HG, N_TILES),
        in_specs=[pl.BlockSpec((TM, kw), lambda g, r: (r, g)),
                  pl.BlockSpec((CTX_T, kw), ctx_t), pl.BlockSpec((CTX_T, vw), ctx_t),
                  pl.BlockSpec((LAT_T, kw), lat_t), pl.BlockSpec((LAT_T, vw), lat_t),
                  pl.BlockSpec((PAST_LEN, kw), past_t), pl.BlockSpec((PAST_LEN, vw), past_t)],
        out_specs=pl.BlockSpec((TM, vw), lambda g, r: (r, g)),
        compiler_params=_cp(2, VMEM_LIMIT),
    )(q, k_tok, v_tok, k_tok, v_tok, k_past, v_past)


def _res_linear_kernel(a_ref, w_ref, x_ref, mod_ref, out_ref):
    out_ref[...] = x_ref[...] + mod_ref[0, 2:3, :] * _dot(a_ref[...], w_ref[...])


def _res_linear(a, w, x, modt):
    k = a.shape[1]
    tile = lambda i: (i, 0)
    return pl.pallas_call(
        _res_linear_kernel,
        out_shape=jax.ShapeDtypeStruct((N_TOK, D), F32),
        grid=(N_TILES,),
        in_specs=[pl.BlockSpec((TM, k), tile), pl.BlockSpec((k, D), lambda i: (0, 0)),
                  pl.BlockSpec((TM, D), tile), pl.BlockSpec((1, 6, D), lambda i: (i, 0, 0))],
        out_specs=pl.BlockSpec((TM, D), tile),
        compiler_params=_cp(1, VMEM_LIMIT),
    )(a, w, x, modt)


def _dot_split(a_hi, a_lo, b_hi, b_lo):
    return _dot(a_hi, b_hi) + (_dot(a_hi, b_lo) + _dot(a_lo, b_hi))


def _fn_channel_dft(x, g, shift, scale, wc_hi, wc_lo):
    h_hi, h_lo = _split2(_norm_mod(x, g, shift, scale))
    a_parts, b_parts = [], []
    for grp in range(FN_GROUPS):
        sl = slice(grp * FN_GW, (grp + 1) * FN_GW)
        ab = _dot_split(h_hi[:, sl], h_lo[:, sl], wc_hi, wc_lo)
        a_parts.append(ab[:, 0:FN_GW])
        b_parts.append(ab[:, FN_GW:2 * FN_GW])
    return jnp.concatenate(a_parts, axis=1), jnp.concatenate(b_parts, axis=1)


FN_STEPS = CTX_TILES + N_LAT_SEQ * 2 * LAT_TILES


def _fn_step(t):
    u = jnp.maximum(t - CTX_TILES, 0)
    seq, ph, tile = u // (2 * LAT_TILES), (u % (2 * LAT_TILES)) // LAT_TILES, u % LAT_TILES
    is_ctx = t < CTX_TILES
    return is_ctx, ph, tile, jnp.where(is_ctx, t, CTX_TILES + seq * LAT_TILES + tile)


def _fn_kernel(x_ref, mod_ref, g_ref, wc_ref, tc_ref, tl_ref, w_ref, out_ref, ab_hi_s, ab_lo_s):
    is_ctx, ph, tile, _ = _fn_step(pl.program_id(0))
    g = g_ref[...]
    shift, scale, gate = mod_ref[0, 0:1, :], mod_ref[0, 1:2, :], mod_ref[0, 2:3, :]
    wc_hi, wc_lo = _split2(wc_ref[...])

    def finish(f, t_len):
        f = f * ((t_len * FN_GW) ** -0.5)
        out_ref[...] = x_ref[...] + gate * _dot(f.astype(BF16), w_ref[...])

    @pl.when(is_ctx)
    def _():
        a, b = _fn_channel_dft(x_ref[...], g, shift, scale, wc_hi, wc_lo)
        ab_hi, ab_lo = _split2(jnp.concatenate([a, b], axis=0))
        finish(_dot_split(*_split2(tc_ref[...]), ab_hi, ab_lo), CTX_T)

    @pl.when(jnp.logical_not(is_ctx) & (ph == 0))
    def _():
        a, b = _fn_channel_dft(x_ref[...], g, shift, scale, wc_hi, wc_lo)
        a_hi, a_lo = _split2(a)
        b_hi, b_lo = _split2(b)
        rows = pl.ds(pl.multiple_of(tile * TM, TM), TM)
        rows_b = pl.ds(pl.multiple_of(LAT_T + tile * TM, TM), TM)
        ab_hi_s[rows, :] = a_hi
        ab_lo_s[rows, :] = a_lo
        ab_hi_s[rows_b, :] = b_hi
        ab_lo_s[rows_b, :] = b_lo

    @pl.when(jnp.logical_not(is_ctx) & (ph == 1))
    def _():
        finish(_dot_split(*_split2(tl_ref[...]), ab_hi_s[...], ab_lo_s[...]), LAT_T)


def _fourier(x, modt, norm_g, tabs, w_out):
    wc, tc, tl = tabs
    const = lambda t: (0, 0)
    x_tile = lambda t: (_fn_step(t)[3], 0)

    def out_tile(t):
        is_ctx, ph, tile, gt = _fn_step(t)
        return (jnp.where(is_ctx | (ph == 1), gt, gt - tile), 0)

    def tab_row(t):
        is_ctx, ph, tile, _ = _fn_step(t)
        return (jnp.where(is_ctx | (ph == 0), 0, tile), 0)

    return pl.pallas_call(
        _fn_kernel,
        out_shape=jax.ShapeDtypeStruct((N_TOK, D), F32),
        grid=(FN_STEPS,),
        in_specs=[pl.BlockSpec((TM, D), x_tile),
                  pl.BlockSpec((1, 6, D), lambda t: (_fn_step(t)[3], 0, 0)),
                  pl.BlockSpec((1, D), const),
                  pl.BlockSpec((FN_GW, 2 * FN_GW), const),
                  pl.BlockSpec((CTX_T, 2 * CTX_T), const),
                  pl.BlockSpec((TM, 2 * LAT_T), tab_row),
                  pl.BlockSpec((D, D), const)],
        out_specs=pl.BlockSpec((TM, D), out_tile),
        scratch_shapes=[pltpu.VMEM((2 * LAT_T, D), BF16), pltpu.VMEM((2 * LAT_T, D), BF16)],
        compiler_params=_cp(1, VMEM_LIMIT),
    )(x, modt, norm_g, wc, tc, tl, w_out)


@functools.lru_cache(maxsize=None)
def _dft_tables():
    def cos_sin(n):
        k = np.arange(n, dtype=np.int64)
        ang = ((k[:, None] * k[None, :]) % n).astype(np.float64) * (2.0 * np.pi / n)
        return np.cos(ang), np.sin(ang)

    cc, sc = cos_sin(FN_GW)
    out = [np.concatenate([cc, sc], axis=1).astype(np.float32)]
    for t_len in (CTX_T, LAT_T):
        ct, st = cos_sin(t_len)
        out.append(np.concatenate([ct, -st], axis=1).astype(np.float32))
    return tuple(out)


def _router_kernel(x_ref, mod_ref, g_ref, rw_hi_ref, rw_lo_ref, rb_ref, h_ref, idt_ref, gate_ref, cnt_ref):
    h = _norm_mod(x_ref[...], g_ref[...], mod_ref[0, 3:4, :], mod_ref[0, 4:5, :])
    _store_packed_rows(h_ref, h)
    h_hi, h_lo = _split2(h)
    logits = _dot_split(h_hi, h_lo, rw_hi_ref[...], rw_lo_ref[...]) + rb_ref[...]
    lane = lax.broadcasted_iota(jnp.int32, logits.shape, 1)
    lane_f = lane.astype(F32)
    cur = jnp.where(lane < N_EXPERTS, logits, -jnp.inf)
    vals, idxs = [], []
    for _ in range(TOP_K):
        m = jnp.max(cur, axis=1, keepdims=True)
        ik = jnp.min(jnp.where(cur == m, lane_f, 128.0), axis=1, keepdims=True).astype(jnp.int32)
        vals.append(m)
        idxs.append(ik)
        cur = jnp.where(lane == ik, -jnp.inf, cur)
    es = [jnp.exp(v - vals[0]) for v in vals]
    denom = functools.reduce(lambda a, b: a + b, es)
    idx_out = jnp.zeros(logits.shape, F32)
    gate_out = jnp.zeros(logits.shape, F32)
    for kk in range(TOP_K):
        idx_out = jnp.where(lane == kk, idxs[kk].astype(F32), idx_out)
        gate_out = jnp.where(lane == kk, es[kk] / denom, gate_out)
    idt_ref[...] = jnp.transpose(idx_out)[0:8, :].astype(jnp.int32)
    gate_ref[...] = gate_out

    @pl.when(pl.program_id(0) == 0)
    def _():
        cnt_ref[...] = jnp.zeros(cnt_ref.shape, F32)

    hits = functools.reduce(lambda a, b: a + b, [jnp.where(lane == ik, 1.0, 0.0) for ik in idxs])
    cnt_ref[...] += jnp.sum(hits, axis=0, keepdims=True)


def _router(x, modt, norm_g, rw_hi, rw_lo, rb):
    tile = lambda i: (i, 0)
    const = lambda i: (0, 0)
    return pl.pallas_call(
        _router_kernel,
        out_shape=(jax.ShapeDtypeStruct((N_TOK * PACK_CHUNKS, 128), jnp.uint32),
                   jax.ShapeDtypeStruct((8, N_TOK), jnp.int32),
                   jax.ShapeDtypeStruct((N_TOK, 128), F32),
                   jax.ShapeDtypeStruct((1, 128), F32)),
        grid=(N_TILES,),
        in_specs=[pl.BlockSpec((TM, D), tile), pl.BlockSpec((1, 6, D), lambda i: (i, 0, 0)),
                  pl.BlockSpec((1, D), const), pl.BlockSpec((D, 128), const), pl.BlockSpec((D, 128), const),
                  pl.BlockSpec((1, 128), const)],
        out_specs=(pl.BlockSpec((TM * PACK_CHUNKS, 128), tile), pl.BlockSpec((8, TM), lambda i: (0, i)),
                   pl.BlockSpec((TM, 128), tile), pl.BlockSpec((1, 128), const)),
        compiler_params=_cp(1, VMEM_LIMIT),
    )(x, modt, norm_g, rw_hi, rw_lo, rb)


def _route_plan(idt, cnt):
    counts = cnt[0, 0:N_EXPERTS].astype(jnp.int32)
    nblk = (counts + MOE_BM - 1) // MOE_BM
    total = jnp.sum(nblk)
    pad = nblk * MOE_BM - counts
    experts = jnp.arange(N_EXPERTS, dtype=jnp.int32)
    filler = jnp.where(jnp.arange(MOE_BM - 1, dtype=jnp.int32)[None, :] < pad[:, None], experts[:, None], N_EXPERTS)
    n_tail = MOE_STEPS * MOE_BM - N_ASSIGN - N_EXPERTS * (MOE_BM - 1)
    keys = jnp.concatenate([idt[0:TOP_K].reshape(-1), filler.reshape(-1), jnp.full((n_tail,), N_EXPERTS, jnp.int32)])
    packed = lax.sort(keys * 65536 + jnp.arange(keys.shape[0], dtype=jnp.int32))
    order = packed & 65535
    b = jnp.arange(MOE_STEPS, dtype=jnp.int32)
    be = packed[::MOE_BM] >> 16
    be = jnp.where(b < total, be, be[jnp.maximum(total - 1, 0)])
    first = (((b == 0) | (be != jnp.concatenate([be[:1], be[:-1]]))) & (b < total)).astype(jnp.int32)
    return be, first, total.reshape(1), order


def _expert_kernel(be_ref, first_ref, total_ref, order_ref,
                   x_hbm, w1_ref, b1_ref, w2_ref, b2_ref, y4_hbm,
                   xs_buf, y_buf, y_acc, xs_bf, w1b, w2b, gsem, ssem):
    b = pl.program_id(0)
    total = total_ref[0]

    def tile_rows(r, per_row=LANE_CHUNKS):
        return pl.ds(pl.multiple_of(r * per_row, per_row), per_row)

    def slot_rows(slot, per_row=LANE_CHUNKS):
        return pl.ds(pl.multiple_of(slot * (MOE_BM * per_row), MOE_BM * per_row), MOE_BM * per_row)

    def gather_start(pos0, slot, row0, i):
        a = order_ref[pos0 + row0 + i]
        pltpu.make_async_copy(x_hbm.at[tile_rows(a & (N_TOK - 1), PACK_CHUNKS), :],
                              xs_buf.at[tile_rows(slot * MOE_BM + row0 + i, PACK_CHUNKS), :], gsem.at[slot]).start()

    def scatter_start(pos0, slot, row0, i):
        row = row0 + i
        a = order_ref[pos0 + row]
        dst = jnp.where(a < N_ASSIGN, a, N_ASSIGN + slot * MOE_BM + row)
        pltpu.make_async_copy(y_buf.at[tile_rows(slot * MOE_BM + row), :], y4_hbm.at[tile_rows(dst), :],
                              ssem.at[slot]).start(priority=1)

    def wait_gather(slot):
        pltpu.make_async_copy(x_hbm.at[slot_rows(0, PACK_CHUNKS), :], xs_buf.at[slot_rows(slot, PACK_CHUNKS), :],
                              gsem.at[slot]).wait()

    def wait_scatter(slot):
        pltpu.make_async_copy(y_buf.at[slot_rows(slot), :], y4_hbm.at[slot_rows(0), :], ssem.at[slot]).wait()

    def row_loop(fn):
        def body(g, c):
            for i in range(8):
                fn(g * 8, i)
            return c
        lax.fori_loop(0, MOE_BM // 8, body, 0)

    @pl.when(b == 0)
    def _():
        y_buf[...] = jnp.zeros(y_buf.shape, F32)
        for s in range(2):
            pltpu.make_async_copy(y_buf.at[slot_rows(s), :], y4_hbm.at[slot_rows(N_ASSIGN // MOE_BM + s), :],
                                  ssem.at[s]).start()
        row_loop(functools.partial(gather_start, 0, 0))
        row_loop(functools.partial(gather_start, jnp.minimum(1, total - 1) * MOE_BM, 1))

    @pl.when(b < total)
    def _():
        slot = b % MOE_SLOTS

        @pl.when(first_ref[b] == 1)
        def _():
            for j in range(2 * MOE_NHC):
                w1b[j] = w1_ref[0, 0, :, j * MOE_HC:(j + 1) * MOE_HC].astype(BF16)
            for j in range(MOE_NHC):
                w2b[j] = w2_ref[0, 0, j * MOE_HC:(j + 1) * MOE_HC, :].astype(BF16)

        wait_gather(slot)
        wait_scatter(slot)
        _load_packed_rows(xs_buf, slot * MOE_BM, MOE_BM, xs_bf)
        y_acc[...] = jnp.broadcast_to(b2_ref[0, 0], (MOE_BM, D))

        npos = jnp.minimum(b + 2, total - 1) * MOE_BM
        ppos = jnp.where(b >= 1, b - 1, MOE_STEPS - 1) * MOE_BM
        oslot = (b + 2) % MOE_SLOTS

        def hidden_chunk(j, c):
            xb = xs_bf[...]
            hg = _dot(xb, w1b[j]) + b1_ref[0, 0, pl.ds(j, 1), :]
            hu = _dot(xb, w1b[MOE_NHC + j]) + b1_ref[0, 0, pl.ds(MOE_NHC + j, 1), :]
            gate = jnp.minimum(hg, SWIGLU_LIMIT)
            up = jnp.clip(hu, -SWIGLU_LIMIT, SWIGLU_LIMIT)
            act = ((up + 1.0) * (gate * _sigmoid(SWIGLU_ALPHA * gate))).astype(BF16)
            y_acc[...] += _dot(act, w2b[j])
            for i in range(MOE_RPC):
                gather_start(npos, oslot, j * MOE_RPC, i)
                scatter_start(ppos, oslot, j * MOE_RPC, i)
            return c
        lax.fori_loop(0, MOE_NHC, hidden_chunk, 0)
        _store_token_tiles(y_buf, slot * MOE_BM, y_acc[...])

    @pl.when(b == total)
    def _():
        last = total - 1
        row_loop(functools.partial(scatter_start, last * MOE_BM, last % MOE_SLOTS))

    @pl.when(b == MOE_STEPS - 1)
    def _():
        for s in range(MOE_SLOTS):
            wait_scatter(s)
        wait_gather(total % MOE_SLOTS)
        wait_gather((total + 1) % MOE_SLOTS)


def _experts(layer, plan, h2, w1, b1, w2, b2):
    be, first, total, order = plan
    wmap = lambda b, be, *_: (layer, be[b], 0, 0)
    grid_spec = pltpu.PrefetchScalarGridSpec(
        num_scalar_prefetch=4,
        grid=(MOE_STEPS,),
        in_specs=[pl.BlockSpec(memory_space=pl.ANY),
                  pl.BlockSpec((1, 1, D, 2 * D), wmap), pl.BlockSpec((1, 1, 2 * MOE_NHC, MOE_HC), wmap),
                  pl.BlockSpec((1, 1, D, D), wmap), pl.BlockSpec((1, 1, 1, D), wmap)],
        out_specs=pl.BlockSpec(memory_space=pl.ANY),
        scratch_shapes=[pltpu.VMEM((MOE_SLOTS * MOE_BM * PACK_CHUNKS, 128), jnp.uint32),
                        pltpu.VMEM((MOE_SLOTS * MOE_BM * LANE_CHUNKS, 128), F32),
                        pltpu.VMEM((MOE_BM, D), F32), pltpu.VMEM((MOE_BM, D), BF16),
                        pltpu.VMEM((2 * MOE_NHC, D, MOE_HC), BF16), pltpu.VMEM((MOE_NHC, MOE_HC, D), BF16),
                        pltpu.SemaphoreType.DMA((MOE_SLOTS,)), pltpu.SemaphoreType.DMA((MOE_SLOTS,))],
    )
    return pl.pallas_call(
        _expert_kernel,
        out_shape=jax.ShapeDtypeStruct((Y4_ROWS * LANE_CHUNKS, 128), F32),
        grid_spec=grid_spec,
        compiler_params=_cp(1, VMEM_LIMIT, disable_bounds_checks=True),
    )(be, first, total, order, h2, w1, b1.reshape(DEPTH, N_EXPERTS, 2 * MOE_NHC, MOE_HC), w2,
      b2.reshape(DEPTH, N_EXPERTS, 1, D))


def _combine_kernel(final, x_ref, y0, y1, y2, y3, gate_ref, mod_ref, fg_ref, out_ref):
    gates = gate_ref[...]
    parts = []
    for c in range(LANE_CHUNKS):
        moe = gates[:, 0:1] * _load_token_tile_chunk(y0, 0, TM, c)
        for kk, y in ((1, y1), (2, y2), (3, y3)):
            moe = moe + gates[:, kk:kk + 1] * _load_token_tile_chunk(y, 0, TM, c)
        parts.append(moe)
    x = x_ref[...] + mod_ref[0, 5:6, :] * jnp.concatenate(parts, axis=1)
    out_ref[...] = _rms(x) * fg_ref[...] if final else x


def _combine(x, y4, gate, modt, final_g, final):
    tile = lambda i: (i, 0)
    plane = lambda kk: pl.BlockSpec((TM * LANE_CHUNKS, 128), lambda i: (kk * N_TILES + i, 0))
    return pl.pallas_call(
        functools.partial(_combine_kernel, final),
        out_shape=jax.ShapeDtypeStruct((N_TOK, D), F32),
        grid=(N_TILES,),
        in_specs=[pl.BlockSpec((TM, D), tile), plane(0), plane(1), plane(2), plane(3),
                  pl.BlockSpec((TM, 128), tile), pl.BlockSpec((1, 6, D), lambda i: (i, 0, 0)),
                  pl.BlockSpec((1, D), lambda i: (0, 0))],
        out_specs=pl.BlockSpec((TM, D), tile),
        compiler_params=_cp(1, VMEM_LIMIT),
    )(x, y4, y4, y4, y4, gate, modt, final_g)


def _moe(layer, x, modt, norm2_g, router_w, router_b, w1, b1, w2, b2, final_g, final):
    rw = jnp.pad(router_w, ((0, 0), (0, 128 - N_EXPERTS)))
    rw_hi, rw_lo = _split2(rw)
    rb = jnp.pad(router_b, (0, 128 - N_EXPERTS)).reshape(1, 128)
    h2, idt, gate, cnt = _router(x, modt, norm2_g, rw_hi, rw_lo, rb)
    y4 = _experts(layer, _route_plan(idt, cnt), h2, w1, b1, w2, b2)
    return _combine(x, y4, gate, modt, final_g, final)


def _rope_tables():
    def tables(n_tokens):
        rows = n_tokens // GRID_W
        row = jnp.repeat(jnp.arange(rows, dtype=F32), GRID_W)
        col = jnp.tile(jnp.arange(GRID_W, dtype=F32), rows)
        n_freq = QK_ROPE // 4
        inv = ROPE_THETA ** (-jnp.arange(n_freq, dtype=F32) / n_freq)
        ang = jnp.stack([row[:, None] * inv, col[:, None] * inv], axis=1)
        return jnp.cos(ang), jnp.sin(ang)

    cos, sin = tables(LAT_T)
    cos64 = jnp.concatenate([cos[:, 0], cos[:, 0], cos[:, 1], cos[:, 1]], axis=1)
    sin64 = jnp.concatenate([-sin[:, 0], sin[:, 0], -sin[:, 1], sin[:, 1]], axis=1)
    cos_all = jnp.concatenate([jnp.ones((N_CTX, QK_ROPE), F32)] + [cos64] * N_LAT_SEQ, axis=0)
    sin_all = jnp.concatenate([jnp.zeros((N_CTX, QK_ROPE), F32)] + [sin64] * N_LAT_SEQ, axis=0)
    return jnp.tile(cos_all, (1, 2)), jnp.tile(sin_all, (1, 2))


_PAIR_SWAP = np.concatenate([np.arange(16, 32), np.arange(0, 16), np.arange(48, 64), np.arange(32, 48)])


def _mla_weights(w_in, w_uq, w_ukv):
    kpe = w_in[:, Q_LORA + KV_LORA:]
    pad = jnp.zeros((D, 64), F32)
    w_in2 = jnp.concatenate([w_in[:, :Q_LORA + KV_LORA], kpe, pad, kpe[:, _PAIR_SWAP], pad], axis=1)
    uq = w_uq.reshape(Q_LORA, MLA_HEADS, QK_NOPE + QK_ROPE)
    rope = uq[:, :, QK_NOPE:]
    zpad = jnp.zeros((Q_LORA, MLA_HEADS, 128 - QK_ROPE), F32)
    w_uq2 = jnp.concatenate([uq[:, :, :QK_NOPE].reshape(Q_LORA, NOPE_ALL),
                             jnp.concatenate([rope, zpad], axis=2).reshape(Q_LORA, ROPE_PAD_ALL),
                             jnp.concatenate([rope[:, :, _PAIR_SWAP], zpad], axis=2).reshape(Q_LORA, ROPE_PAD_ALL)],
                            axis=1)
    ukv = w_ukv.reshape(KV_LORA, MLA_HEADS, QK_NOPE + V_HEAD)
    w_ukv2 = jnp.concatenate([ukv[:, :, :QK_NOPE].reshape(KV_LORA, NOPE_ALL),
                              ukv[:, :, QK_NOPE:].reshape(KV_LORA, NOPE_ALL)], axis=1)
    return w_in2.astype(BF16), w_uq2.astype(BF16), w_ukv2.astype(BF16)


def _ml_init_state(state_c, state_n, state_m, j, direction):
    c = state_c[:, j, direction].astype(F32)
    n = state_n[:, j, direction].astype(F32)[..., None]
    m = jnp.broadcast_to(state_m[:, j, direction].astype(F32)[..., None, None], n.shape)
    pad = jnp.zeros(c.shape[:-1] + (ML_AUG - ML_DV - 2,), F32)
    lat = jnp.concatenate([c, n, m, pad], axis=-1)
    return jnp.concatenate([jnp.zeros((N_CTX_SEQ,) + lat.shape[1:], F32), lat], axis=0)


def kernel(x_prompt, x_sample, cache_mla_ckv, cache_mla_kpe, state_mlstm_C, state_mlstm_n, state_mlstm_m, c, c_ctx, norm1_g, norm2_g, ada_w, ada_b, ml_w_in, ml_gate_b, ml_norm_g, ml_w_out, mla_w_in, mla_q_norm_g, mla_w_uq, mla_kv_norm_g, mla_w_ukv, mla_w_out, fn_w_out, router_w, router_b, exp_w1, exp_b1, exp_w2, exp_b2, final_g):
    x = jnp.concatenate([x_prompt.reshape(N_CTX, D), x_sample.reshape(N_LAT_SEQ * LAT_T, D)], axis=0)

    cond8 = jnp.concatenate([c_ctx[None, :], c, jnp.zeros((8 - 1 - N_LAT_SEQ, D), F32)], axis=0)
    mod = _modulation(cond8, ada_w, ada_b)
    tile_cond = np.concatenate([np.zeros(CTX_TILES, np.int32)] +
                               [np.full(LAT_TILES, 1 + s, np.int32) for s in range(N_LAT_SEQ)])
    modt = mod[:, tile_cond].reshape(DEPTH, N_TILES, 6, D)

    hk = ML_HEADS * ML_DK
    states = []
    new_ckv = new_kpe = None
    for l in range(DEPTH):
        kind, j = l % 3, l // 3
        n1 = norm1_g[l].reshape(1, D)
        if kind == 0:
            w = ml_w_in[j]
            w_gates = w[:, 2 * hk + 2 * D:]
            qkv, o, gg, gt = _ml_in(x, modt[l], n1, w[:, :2 * hk + 2 * D].astype(BF16),
                                    jnp.pad(w_gates, ((0, 0), (0, 128 - 4 * ML_HEADS))).astype(BF16),
                                    w_gates.T.astype(BF16))
            gb = ml_gate_b[j].reshape(4 * ML_HEADS).astype(F32)
            gb_row = jnp.pad(gb, (0, 128 - 4 * ML_HEADS)).reshape(1, 128)
            gb_col = jnp.broadcast_to(gb[:, None], (4 * ML_HEADS, 128))
            hf, hb, st_f, st_b = _ml_core(qkv, gg, gt, gb_row, gb_col,
                                          _ml_init_state(state_mlstm_C, state_mlstm_n, state_mlstm_m, j, 0),
                                          _ml_init_state(state_mlstm_C, state_mlstm_n, state_mlstm_m, j, 1))
            states.append((st_f[:N_CTX_SEQ], st_b[:N_CTX_SEQ]))
            x = _ml_out(hf, hb, o, ml_norm_g[j].reshape(1, D), ml_w_out[j].astype(BF16), x, modt[l])
        elif kind == 1:
            w_in2, w_uq2, w_ukv2 = _mla_weights(mla_w_in[j], mla_w_uq[j], mla_w_ukv[j])
            cos2, sin2 = _rope_tables()
            q, ckv, kpe = _mla_in(x, modt[l], n1, w_in2, mla_q_norm_g[j].reshape(1, Q_LORA),
                                  mla_kv_norm_g[j].reshape(1, KV_LORA), w_uq2, cos2, sin2)
            new_ckv, new_kpe = ckv[:N_CTX], kpe[:N_CTX, 0:QK_ROPE]
            k_tok, v_tok = _kv_expand(ckv, kpe, w_ukv2)
            kr_past = jnp.pad(cache_mla_kpe[:, j].reshape(N_LAT_SEQ * PAST_LEN, QK_ROPE), ((0, 0), (0, 128 - QK_ROPE)))
            k_past, v_past = _kv_expand(cache_mla_ckv[:, j].reshape(N_LAT_SEQ * PAST_LEN, KV_LORA), kr_past, w_ukv2)
            att = _attention(q, k_tok, v_tok, k_past, v_past)
            x = _res_linear(att, mla_w_out[j].astype(BF16), x, modt[l])
        else:
            x = _fourier(x, modt[l], n1, _dft_tables(), fn_w_out[j].astype(BF16))
        x = _moe(l, x, modt[l], norm2_g[l].reshape(1, D), router_w[l], router_b[l],
                 exp_w1, exp_b1, exp_w2, exp_b2, final_g.reshape(1, D), l == DEPTH - 1)

    y_prompt = x[:N_CTX].reshape(N_CTX_SEQ, CTX_T, D)
    y_sample = x[N_CTX:].reshape(N_LAT_SEQ, LAT_T, D)
    new_mla_ckv = new_ckv.reshape(N_CTX_SEQ, 1, CTX_T, KV_LORA)
    new_mla_kpe = new_kpe.reshape(N_CTX_SEQ, 1, CTX_T, QK_ROPE)
    st = jnp.stack([jnp.stack([sf, sb], axis=1) for sf, sb in states], axis=1)
    new_c = st[..., 0:ML_DV]
    new_n = st[..., ML_N_LANE]
    new_m = st[..., 0, ML_M_LANE]
    return (y_prompt, y_sample, new_mla_ckv, new_mla_kpe, new_c, new_n, new_m)
```
